```python
import jax, jax.numpy as jnp
from jax import lax
import numpy as np

D_MODEL = 4096
BATCH = 4
SEQ = 4096
DEPTH = 2

HEAD_DIM = 128
N_HEADS = D_MODEL // HEAD_DIM
H_REC = N_HEADS // 2
H_ATT = N_HEADS - H_REC
DK_REC = 128
DV_REC = HEAD_DIM
W_REC = H_REC * DV_REC
W_ATT = H_ATT * HEAD_DIM
MIX_WIDTH = W_REC + W_ATT
D_FF = 4 * D_MODEL
CHUNK = 64
Q_BLOCK = 128
N_MOD = 6
EPS = 1e-6
MASK_VALUE = -1e30
K_MAX = 1.0 - 1e-6

O_QR = H_REC * DK_REC
O_FR = O_QR + H_REC * DK_REC
O_IR = O_FR + W_REC
O_GR = O_IR + W_REC
O_QA = O_GR + W_ATT
O_KA = O_QA + W_ATT
O_VA = O_KA + W_ATT
IN_COLS = O_VA + H_ATT

kernel_name = "hymba_hgrn2_fox_adaln_block"


def _rms_norm(x, w):
    xf = x.astype(jnp.float32)
    y = xf * lax.rsqrt(jnp.mean(xf * xf, axis=-1, keepdims=True) + EPS)
    return (y * w.astype(jnp.float32)).astype(x.dtype)


def _hgrn2(q, f_logit, i, lb):
    B, S = q.shape[0], q.shape[1]
    nc = S // CHUNK
    z = f_logit.astype(jnp.float32)
    lbh = lb.astype(jnp.float32).reshape(H_REC, DK_REC)
    k = (1.0 - lbh) * jax.nn.sigmoid(-z)
    log_f = jnp.log1p(-jnp.minimum(k, K_MAX))
    qf = jax.nn.silu(q.astype(jnp.float32))
    vf = i.astype(jnp.float32)

    def to_chunks(a):
        return a.reshape(B, nc, CHUNK, H_REC, a.shape[-1]).transpose(1, 0, 3, 2, 4)

    mask = jnp.tril(jnp.ones((CHUNK, CHUNK), dtype=bool))[:, :, None]

    def step(state, inp):
        qc, kc, vc, lfc = inp
        G = jnp.cumsum(lfc, axis=2)
        o_inter = jnp.einsum('bhck,bhkv->bhcv', qc * jnp.exp(G), state)
        diff = G[:, :, :, None, :] - G[:, :, None, :, :]
        decay = jnp.where(mask, jnp.exp(jnp.minimum(diff, 0.0)), 0.0)
        A = jnp.einsum('bhtk,bhsk,bhtsk->bhts', qc, kc, decay)
        o_intra = jnp.einsum('bhts,bhsv->bhtv', A, vc)
        G_last = G[:, :, -1:, :]
        new_state = (jnp.exp(G_last[:, :, 0, :])[..., None] * state
                     + jnp.einsum('bhsk,bhsv->bhkv', kc * jnp.exp(G_last - G), vc))
        return new_state, o_inter + o_intra

    s0 = jnp.zeros((B, H_REC, DK_REC, DV_REC), jnp.float32)
    _, o = lax.scan(step, s0, (to_chunks(qf), to_chunks(k), to_chunks(vf), to_chunks(log_f)))
    return o.transpose(1, 0, 3, 2, 4).reshape(B, S, H_REC, DV_REC)


def _fox(q, k, v, fg_logit, fg_bias, qn_w, kn_w):
    B, S = q.shape[0], q.shape[1]
    q = _rms_norm(q, qn_w)
    k = _rms_norm(k, kn_w)
    log_f = jax.nn.log_sigmoid(fg_logit.astype(jnp.float32) + fg_bias.astype(jnp.float32))
    F = jnp.cumsum(log_f, axis=1).transpose(0, 2, 1)
    scale = HEAD_DIM ** -0.5
    outs = []
    for blk in range(S // Q_BLOCK):
        q0, q1 = blk * Q_BLOCK, (blk + 1) * Q_BLOCK
        s = jnp.einsum('bqhd,bkhd->bhqk', q[:, q0:q1], k[:, :q1]).astype(jnp.float32) * scale
        s = s + F[:, :, q0:q1, None] - F[:, :, None, :q1]
        causal = jnp.arange(q0, q1)[:, None] >= jnp.arange(q1)[None, :]
        p = jax.nn.softmax(jnp.where(causal, s, MASK_VALUE), axis=-1).astype(v.dtype)
        outs.append(jnp.einsum('bhqk,bkhd->bqhd', p, v[:, :q1]))
    return jnp.concatenate(outs, axis=1).reshape(B, S, W_ATT)


def setup_inputs(seed: int = 0) -> dict:
    key = jax.random.key(seed)
    ks = jax.random.split(key, 16)
    f32 = jnp.float32
    n = lambda k, shape, s: jax.random.normal(k, shape, f32) * s
    return {
        "x": n(ks[0], (BATCH, SEQ, D_MODEL), 1.0),
        "c": n(ks[1], (BATCH, D_MODEL), 1.0),
        "lower_bounds": n(ks[2], (DEPTH, H_REC * DK_REC), 0.5),
        "w_ada": n(ks[3], (DEPTH, D_MODEL, N_MOD * D_MODEL), 0.5 * D_MODEL ** -0.5),
        "b_ada": n(ks[4], (DEPTH, N_MOD * D_MODEL), 0.02),
        "norm_mix_w": 1.0 + n(ks[5], (DEPTH, D_MODEL), 0.02),
        "norm_ffn_w": 1.0 + n(ks[6], (DEPTH, D_MODEL), 0.02),
        "w_in": n(ks[7], (DEPTH, D_MODEL, IN_COLS), D_MODEL ** -0.5),
        "rec_norm_w": 1.0 + n(ks[8], (DEPTH, DV_REC), 0.02),
        "fg_bias": 1.0 + n(ks[9], (DEPTH, H_ATT), 0.5),
        "q_norm_w": 1.0 + n(ks[10], (DEPTH, HEAD_DIM), 0.02),
        "k_norm_w": 1.0 + n(ks[11], (DEPTH, HEAD_DIM), 0.02),
        "w_out": n(ks[12], (DEPTH, MIX_WIDTH, D_MODEL), MIX_WIDTH ** -0.5),
        "w_up": n(ks[13], (DEPTH, D_MODEL, D_FF), D_MODEL ** -0.5),
        "w_down": n(ks[14], (DEPTH, D_FF, D_MODEL), D_FF ** -0.5),
    }


def reference(x, c, lower_bounds, w_ada, b_ada, norm_mix_w, norm_ffn_w, w_in, rec_norm_w,
              fg_bias, q_norm_w, k_norm_w, w_out, w_up, w_down):
    B, S = x.shape[0], x.shape[1]
    p = jax.nn.softmax(lower_bounds.astype(jnp.float32), axis=0)
    lb_all = jnp.cumsum(p, axis=0) - p[0:1]
    c_act = jax.nn.silu(c)
    for l in range(DEPTH):
        mod = (c_act @ w_ada[l] + b_ada[l]).reshape(B, N_MOD, 1, D_MODEL)
        shift1, scale1, gate1 = mod[:, 0], mod[:, 1], mod[:, 2]
        shift2, scale2, gate2 = mod[:, 3], mod[:, 4], mod[:, 5]

        h = _rms_norm(x, norm_mix_w[l]) * (1.0 + scale1) + shift1
        proj = h @ w_in[l]
        q_r, f_r, i_r, g_r, q_a, k_a, v_a, fg_a = jnp.split(
            proj, [O_QR, O_FR, O_IR, O_GR, O_QA, O_KA, O_VA], axis=-1)

        o_rec = _hgrn2(q_r.reshape(B, S, H_REC, DK_REC), f_r.reshape(B, S, H_REC, DK_REC),
                       i_r.reshape(B, S, H_REC, DV_REC), lb_all[l])
        o_rec = _rms_norm(o_rec, rec_norm_w[l]).reshape(B, S, W_REC).astype(x.dtype)
        o_rec = o_rec * jax.nn.silu(g_r)

        o_att = _fox(q_a.reshape(B, S, H_ATT, HEAD_DIM), k_a.reshape(B, S, H_ATT, HEAD_DIM),
                     v_a.reshape(B, S, H_ATT, HEAD_DIM), fg_a, fg_bias[l], q_norm_w[l], k_norm_w[l])

        mix = jnp.concatenate([o_rec, o_att], axis=-1) @ w_out[l]
        x = x + gate1 * mix

        h2 = _rms_norm(x, norm_ffn_w[l]) * (1.0 + scale2) + shift2
        x = x + gate2 * (jnp.square(jax.nn.relu(h2 @ w_up[l])) @ w_down[l])
    return x
```

```python
import functools

import jax
import jax.numpy as jnp
from jax import lax
from jax.experimental import pallas as pl
from jax.experimental.pallas import tpu as pltpu

HEAD_DIM = 128
CHUNK = 64
SUB = 16
N_SUB = CHUNK // SUB
N_MOD = 6
EPS = 1e-6
MASK_VALUE = -1e30
K_MAX = 1.0 - 1e-6
VMEM_LIMIT_BYTES = 56 * 1024 * 1024

F32 = jnp.float32
BF16 = jnp.bfloat16

_NT = (((1,), (1,)), ((), ()))
_TN = (((0,), (0,)), ((), ()))


def _params(*sem):
    return pltpu.CompilerParams(dimension_semantics=sem, vmem_limit_bytes=VMEM_LIMIT_BYTES)


def _fit_tile(tile, dim):
    tile = min(tile, dim)
    while dim % tile:
        tile -= HEAD_DIM
    return tile


def _sigmoid(x):
    return 1.0 / (1.0 + jnp.exp(-x))


def _tril_cumsum(tril, x):
    hi = x.astype(BF16)
    r1 = x - hi.astype(F32)
    mid = r1.astype(BF16)
    lo = (r1 - mid.astype(F32)).astype(BF16)
    out = jnp.dot(tril, hi, preferred_element_type=F32)
    out = out + jnp.dot(tril, mid, preferred_element_type=F32)
    return out + jnp.dot(tril, lo, preferred_element_type=F32)


def _tril(n):
    r = lax.broadcasted_iota(jnp.int32, (n, n), 0)
    c = lax.broadcasted_iota(jnp.int32, (n, n), 1)
    return (r >= c).astype(BF16)


def _mod_kernel(c_ref, w_ref, b_ref, o_ref):
    c = c_ref[...]
    ca = (c * _sigmoid(c)).astype(BF16)
    w = w_ref[...].astype(BF16)
    o_ref[...] = jnp.dot(ca, w, preferred_element_type=F32) + b_ref[...]


def _adaln_mod(c_pad, w_ada, b_ada):
    depth, d, n = w_ada.shape
    rows = c_pad.shape[0]
    tn = min(512, n)
    return pl.pallas_call(
        _mod_kernel,
        grid=(depth, n // tn),
        in_specs=[
            pl.BlockSpec((rows, d), lambda l, j: (0, 0)),
            pl.BlockSpec((None, d, tn), lambda l, j: (l, 0, j)),
            pl.BlockSpec((None, 1, tn), lambda l, j: (l, 0, j)),
        ],
        out_specs=pl.BlockSpec((None, rows, tn), lambda l, j: (l, 0, j)),
        out_shape=jax.ShapeDtypeStruct((depth, rows, n), F32),
        compiler_params=_params("arbitrary", "arbitrary"),
        name="adaln_mod",
    )(c_pad, w_ada, b_ada.reshape(depth, 1, n))


def _norm_mod_kernel(x_ref, w_ref, scale_ref, shift_ref, o_ref):
    x = x_ref[...]
    ms = jnp.mean(x * x, axis=-1, keepdims=True)
    y = x * lax.rsqrt(ms + EPS) * w_ref[...]
    o_ref[...] = (y * (1.0 + scale_ref[...]) + shift_ref[...]).astype(o_ref.dtype)


def _norm_mod(x3, w, mod6, scale_idx, shift_idx):
    b, s, d = x3.shape
    ts = min(512, s)
    return pl.pallas_call(
        _norm_mod_kernel,
        grid=(b, s // ts),
        in_specs=[
            pl.BlockSpec((None, ts, d), lambda i, t: (i, t, 0)),
            pl.BlockSpec((1, d), lambda i, t: (0, 0)),
            pl.BlockSpec((None, 1, d), lambda i, t: (i * N_MOD + scale_idx, 0, 0)),
            pl.BlockSpec((None, 1, d), lambda i, t: (i * N_MOD + shift_idx, 0, 0)),
        ],
        out_specs=pl.BlockSpec((None, ts, d), lambda i, t: (i, t, 0)),
        out_shape=jax.ShapeDtypeStruct((b, s, d), BF16),
        compiler_params=_params("arbitrary", "arbitrary"),
        name="norm_mod",
    )(x3, w.reshape(1, d), mod6, mod6)


def _mm_kernel(*refs, n_a, nk, mode):
    a_refs = refs[:n_a]
    w_ref = refs[n_a]
    rest = refs[n_a + 1:]
    if mode == "resid":
        x_ref, g_ref, o_ref = rest[:3]
        rest = rest[3:]
    else:
        o_ref = rest[0]
        rest = rest[1:]

    part = None
    off = 0
    for a_ref in a_refs:
        ka = a_ref.shape[1]
        p = jnp.dot(a_ref[...], w_ref[off:off + ka, :], preferred_element_type=F32)
        part = p if part is None else part + p
        off += ka

    def finish(acc):
        if mode == "plain":
            o_ref[...] = acc.astype(o_ref.dtype)
        elif mode == "relu2":
            r = jnp.maximum(acc, 0.0)
            o_ref[...] = (r * r).astype(o_ref.dtype)
        else:
            o_ref[...] = x_ref[...] + g_ref[...] * acc

    if nk == 1:
        finish(part)
    else:
        acc_ref = rest[0]
        k = pl.program_id(2)

        @pl.when(k == 0)
        def _():
            acc_ref[...] = part

        @pl.when(jnp.logical_and(k > 0, k < nk - 1))
        def _():
            acc_ref[...] += part

        @pl.when(k == nk - 1)
        def _():
            finish(acc_ref[...] + part)


def _matmul(a_list, w, *, mode, out_dtype, tm, tn, tk=None, resid=None, rows_per_batch=None):
    m = a_list[0].shape[0]
    kdim, n = w.shape
    n_a = len(a_list)
    tm = _fit_tile(tm, m)
    tn = _fit_tile(tn, n)
    if tk is None or tk >= kdim:
        tk = kdim
    nk = kdim // tk
    assert n_a == 1 or nk == 1
    assert m % tm == 0 and n % tn == 0 and kdim % tk == 0

    in_specs = []
    if n_a == 1:
        in_specs.append(pl.BlockSpec((tm, tk), lambda i, j, k: (i, k)))
    else:
        for a in a_list:
            in_specs.append(pl.BlockSpec((tm, a.shape[1]), lambda i, j, k: (i, 0)))
    in_specs.append(pl.BlockSpec((tk, tn), lambda i, j, k: (k, j)))
    args = list(a_list) + [w]
    if mode == "resid":
        x2, mod6, gate_idx = resid
        assert rows_per_batch % tm == 0
        tiles_per_batch = rows_per_batch // tm
        in_specs.append(pl.BlockSpec((tm, tn), lambda i, j, k: (i, j)))
        in_specs.append(pl.BlockSpec(
            (None, 1, tn), lambda i, j, k: ((i // tiles_per_batch) * N_MOD + gate_idx, 0, j)))
        args += [x2, mod6]
    scratch = [pltpu.VMEM((tm, tn), F32)] if nk > 1 else []
    return pl.pallas_call(
        functools.partial(_mm_kernel, n_a=n_a, nk=nk, mode=mode),
        grid=(m // tm, n // tn, nk),
        in_specs=in_specs,
        out_specs=pl.BlockSpec((tm, tn), lambda i, j, k: (i, j)),
        out_shape=jax.ShapeDtypeStruct((m, n), out_dtype),
        scratch_shapes=scratch,
        compiler_params=_params("arbitrary", "arbitrary", "arbitrary"),
        name="mm_" + mode,
    )(*args)


def _fcum_kernel(fg_ref, bias_ref, fc_ref, fr_ref):
    n_slab = fg_ref.shape[0] // HEAD_DIM
    tril = _tril(HEAD_DIM)

    def body(r, carry):
        rows = pl.ds(pl.multiple_of(r * HEAD_DIM, HEAD_DIM), HEAD_DIM)
        z = fg_ref[rows, :] + bias_ref[...]
        lf = jnp.minimum(z, 0.0) - jnp.log1p(jnp.exp(-jnp.abs(z)))
        f = _tril_cumsum(tril, lf) + carry
        fc_ref[rows, :] = f
        fr_ref[r] = f.T
        return f[HEAD_DIM - 1:HEAD_DIM, :]

    lax.fori_loop(0, n_slab, body, jnp.zeros((1, HEAD_DIM), F32))


def _fcum(fg3, bias_row):
    b, s, _ = fg3.shape
    n_slab = s // HEAD_DIM
    return pl.pallas_call(
        _fcum_kernel,
        grid=(b,),
        in_specs=[
            pl.BlockSpec((None, s, HEAD_DIM), lambda i: (i, 0, 0)),
            pl.BlockSpec((1, HEAD_DIM), lambda i: (0, 0)),
        ],
        out_specs=[
            pl.BlockSpec((None, s, HEAD_DIM), lambda i: (i, 0, 0)),
            pl.BlockSpec((None, n_slab, HEAD_DIM, HEAD_DIM), lambda i: (i, 0, 0, 0)),
        ],
        out_shape=[
            jax.ShapeDtypeStruct((b, s, HEAD_DIM), F32),
            jax.ShapeDtypeStruct((b, n_slab, HEAD_DIM, HEAD_DIM), F32),
        ],
        compiler_params=_params("arbitrary"),
        name="fox_fcum",
    )(fg3, bias_row)


def _fox_kernel(q_ref, k_ref, v_ref, fc_ref, fr_ref, qn_ref, kn_ref, o_ref,
                kn_s, vb_s, m_s, l_s, acc_s, *, blk):
    h = pl.program_id(1)
    qi = pl.program_id(2)

    @pl.when(qi == 0)
    def _():
        k = k_ref[...]
        ms = jnp.mean(k * k, axis=-1, keepdims=True)
        kn_s[...] = (k * lax.rsqrt(ms + EPS) * kn_ref[...]).astype(BF16)
        vb_s[...] = v_ref[...].astype(BF16)

    q = q_ref[...]
    ms = jnp.mean(q * q, axis=-1, keepdims=True)
    qn = (q * lax.rsqrt(ms + EPS) * qn_ref[...] * (HEAD_DIM ** -0.5)).astype(BF16)
    lane = lax.broadcasted_iota(jnp.int32, (blk, HEAD_DIM), 1)
    fq = jnp.sum(jnp.where(lane == h, fc_ref[...], 0.0), axis=1, keepdims=True)

    m_s[...] = jnp.full(m_s.shape, MASK_VALUE, F32)
    l_s[...] = jnp.zeros(l_s.shape, F32)
    acc_s[...] = jnp.zeros(acc_s.shape, F32)

    def step(kj, causal):
        rows = pl.ds(pl.multiple_of(kj * blk, blk), blk)
        s = lax.dot_general(qn, kn_s[rows, :], _NT, preferred_element_type=F32)
        s = s + (fq - fr_ref[kj])
        if causal:
            r = lax.broadcasted_iota(jnp.int32, (blk, blk), 0)
            c = lax.broadcasted_iota(jnp.int32, (blk, blk), 1)
            s = jnp.where(r >= c, s, MASK_VALUE)
        m_old = m_s[...]
        m_new = jnp.maximum(m_old, jnp.max(s, axis=1, keepdims=True))
        p = jnp.exp(s - m_new)
        alpha = jnp.exp(m_old - m_new)
        l_s[...] = alpha * l_s[...] + jnp.sum(p, axis=1, keepdims=True)
        acc_s[...] = alpha * acc_s[...] + jnp.dot(p.astype(BF16), vb_s[rows, :],
                                                  preferred_element_type=F32)
        m_s[...] = m_new

    def body(kj, carry):
        step(kj, False)
        return carry

    lax.fori_loop(0, qi, body, 0)
    step(qi, True)
    o_ref[...] = (acc_s[...] / l_s[...]).astype(o_ref.dtype)


def _fox(proj3, fc, fr5, qn_w, kn_w, *, n_heads, q_blk0, k_blk0, v_blk0, blk):
    b, s, _ = proj3.shape
    nq = s // blk
    return pl.pallas_call(
        functools.partial(_fox_kernel, blk=blk),
        grid=(b, n_heads, nq),
        in_specs=[
            pl.BlockSpec((None, blk, HEAD_DIM), lambda i, h, t: (i, t, q_blk0 + h)),
            pl.BlockSpec((None, s, HEAD_DIM), lambda i, h, t: (i, 0, k_blk0 + h)),
            pl.BlockSpec((None, s, HEAD_DIM), lambda i, h, t: (i, 0, v_blk0 + h)),
            pl.BlockSpec((None, blk, HEAD_DIM), lambda i, h, t: (i, t, 0)),
            pl.BlockSpec((None, None, nq, 1, blk), lambda i, h, t: (i, h, 0, 0, 0)),
            pl.BlockSpec((1, HEAD_DIM), lambda i, h, t: (0, 0)),
            pl.BlockSpec((1, HEAD_DIM), lambda i, h, t: (0, 0)),
        ],
        out_specs=pl.BlockSpec((None, blk, HEAD_DIM), lambda i, h, t: (i, t, h)),
        out_shape=jax.ShapeDtypeStruct((b, s, n_heads * HEAD_DIM), BF16),
        scratch_shapes=[
            pltpu.VMEM((s, HEAD_DIM), BF16),
            pltpu.VMEM((s, HEAD_DIM), BF16),
            pltpu.VMEM((blk, 1), F32),
            pltpu.VMEM((blk, 1), F32),
            pltpu.VMEM((blk, HEAD_DIM), F32),
        ],
        compiler_params=_params("arbitrary", "arbitrary", "arbitrary"),
        name="fox_attn",
    )(proj3, proj3, proj3, fc, fr5, qn_w.reshape(1, HEAD_DIM), kn_w.reshape(1, HEAD_DIM))


def _hgrn2_kernel(q_ref, f_ref, i_ref, g_ref, lb_ref, nw_ref, seg_ref, o_ref,
                  st_s, y_s, g_s, kk_s, qf_s, r_s, *, layer, rows):
    t_id = pl.program_id(2)
    n_chunk = rows // CHUNK

    @pl.when(t_id == 0)
    def _():
        st_s[...] = jnp.zeros(st_s.shape, F32)

    depth = lb_ref.shape[0]
    lbs = [lb_ref[j] for j in range(depth)]
    mx = lbs[0]
    for j in range(1, depth):
        mx = jnp.maximum(mx, lbs[j])
    es = [jnp.exp(v - mx) for v in lbs]
    den = es[0]
    for j in range(1, depth):
        den = den + es[j]
    lb = jnp.zeros((1, HEAD_DIM), F32)
    for j in range(1, layer + 1):
        lb = lb + es[j] / den

    tril = _tril(CHUNK)

    def prep(c, carry):
        rws = pl.ds(pl.multiple_of(c * CHUNK, CHUNK), CHUNK)
        z = f_ref[rws, :]
        kk = (1.0 - lb) * _sigmoid(-z)
        logf = jnp.log1p(-jnp.minimum(kk, K_MAX))
        gc = _tril_cumsum(tril, logf)
        q = q_ref[rws, :]
        qf = q * _sigmoid(q)
        kk_s[rws, :] = kk
        g_s[rws, :] = gc
        qf_s[rws, :] = qf
        for b in range(N_SUB):
            qb = qf[b * SUB:(b + 1) * SUB]
            kb = kk[b * SUB:(b + 1) * SUB]
            gb = gc[b * SUB:(b + 1) * SUB]
            yrows = pl.ds(pl.multiple_of(c * CHUNK + b * SUB, SUB), SUB)
            for s in range(SUB):
                e = jnp.exp(jnp.minimum(gb - gb[s:s + 1], 0.0))
                y_s[yrows, s * HEAD_DIM:(s + 1) * HEAD_DIM] = (qb * kb[s:s + 1] * e).astype(BF16)
        return carry

    lax.fori_loop(0, n_chunk, prep, 0)

    r_s[...] = jnp.dot(y_s[...], seg_ref[...], preferred_element_type=F32)

    tr = lax.broadcasted_iota(jnp.int32, (CHUNK, CHUNK), 0)
    tc = lax.broadcasted_iota(jnp.int32, (CHUNK, CHUNK), 1)
    same_blk = (tr // SUB) == (tc // SUB)
    diag_mask = jnp.logical_and(same_blk, tc <= tr)
    off_mask = (tc // SUB) < (tr // SUB)

    def chunk(c, carry):
        rws = pl.ds(pl.multiple_of(c * CHUNK, CHUNK), CHUNK)
        qf = qf_s[rws, :]
        kk = kk_s[rws, :]
        gc = g_s[rws, :]
        v = i_ref[rws, :]
        vb = v.astype(BF16)
        g_last = gc[CHUNK - 1:CHUNK]
        blocks = [jnp.zeros((SUB, CHUNK), F32)]
        for i in range(1, N_SUB):
            gn = gc[i * SUB:i * SUB + 1]
            ql = qf[i * SUB:(i + 1) * SUB] * jnp.exp(gc[i * SUB:(i + 1) * SUB] - gn)
            kr = kk * jnp.exp(jnp.minimum(gn - gc, 0.0))
            blocks.append(lax.dot_general(ql.astype(BF16), kr.astype(BF16), _NT,
                                          preferred_element_type=F32))
        a_off = jnp.concatenate(blocks, axis=0)
        a = jnp.where(diag_mask, r_s[rws, :][:, :CHUNK], jnp.where(off_mask, a_off, 0.0))
        st = st_s[...]
        qe = qf * jnp.exp(gc)
        o = jnp.dot(a.astype(BF16), vb, preferred_element_type=F32)
        o = o + lax.dot_general(qe.astype(BF16), st.astype(BF16), _NT,
                                preferred_element_type=F32)
        kd = kk * jnp.exp(g_last - gc)
        st_s[...] = jnp.exp(g_last) * st + lax.dot_general(
            vb, kd.astype(BF16), _TN, preferred_element_type=F32)
        ms = jnp.mean(o * o, axis=-1, keepdims=True)
        y = o * lax.rsqrt(ms + EPS) * nw_ref[...]
        g = g_ref[rws, :]
        o_ref[rws, :] = (y * (g * _sigmoid(g))).astype(o_ref.dtype)
        return carry

    lax.fori_loop(0, n_chunk, chunk, 0)


def _hgrn2(proj3, lower_bounds, nw, *, layer, n_heads, rows):
    b, s, _ = proj3.shape
    depth = lower_bounds.shape[0]
    rows = min(rows, s)
    seg_r = jnp.arange(SUB * HEAD_DIM, dtype=jnp.int32)[:, None] // HEAD_DIM
    seg_c = jnp.arange(HEAD_DIM, dtype=jnp.int32)[None, :]
    seg = jnp.logical_and(seg_c < CHUNK, (seg_c % SUB) == seg_r).astype(BF16)
    lb4 = lower_bounds.astype(F32).reshape(depth, n_heads, 1, HEAD_DIM)

    def col(off):
        return pl.BlockSpec((None, rows, HEAD_DIM), lambda i, h, t: (i, t, off * n_heads + h))

    return pl.pallas_call(
        functools.partial(_hgrn2_kernel, layer=layer, rows=rows),
        grid=(b, n_heads, s // rows),
        in_specs=[
            col(0), col(1), col(2), col(3),
            pl.BlockSpec((depth, None, 1, HEAD_DIM), lambda i, h, t: (0, h, 0, 0)),
            pl.BlockSpec((1, HEAD_DIM), lambda i, h, t: (0, 0)),
            pl.BlockSpec((SUB * HEAD_DIM, HEAD_DIM), lambda i, h, t: (0, 0)),
        ],
        out_specs=pl.BlockSpec((None, rows, HEAD_DIM), lambda i, h, t: (i, t, h)),
        out_shape=jax.ShapeDtypeStruct((b, s, n_heads * HEAD_DIM), BF16),
        scratch_shapes=[
            pltpu.VMEM((HEAD_DIM, HEAD_DIM), F32),
            pltpu.VMEM((rows, SUB * HEAD_DIM), BF16),
            pltpu.VMEM((rows, HEAD_DIM), F32),
            pltpu.VMEM((rows, HEAD_DIM), F32),
            pltpu.VMEM((rows, HEAD_DIM), F32),
            pltpu.VMEM((rows, HEAD_DIM), F32),
        ],
        compiler_params=_params("arbitrary", "arbitrary", "arbitrary"),
        name="hgrn2",
    )(proj3, proj3, proj3, proj3, lb4, nw.reshape(1, HEAD_DIM), seg)


def kernel(x, c, lower_bounds, w_ada, b_ada, norm_mix_w, norm_ffn_w, w_in, rec_norm_w,
           fg_bias, q_norm_w, k_norm_w, w_out, w_up, w_down):
    b, s, d = x.shape
    depth = w_ada.shape[0]
    n_heads = d // HEAD_DIM
    h_rec = n_heads // 2
    h_att = n_heads - h_rec
    main_cols = (4 * h_rec + 3 * h_att) * HEAD_DIM
    assert w_in.shape[2] == main_cols + h_att and h_att <= HEAD_DIM
    assert s % CHUNK == 0 and s % HEAD_DIM == 0
    m = b * s
    fox_blk = min(512, s)

    rows_pad = -(-b // 8) * 8
    c_pad = jnp.zeros((rows_pad, d), F32).at[:b].set(c.astype(F32))
    mod = _adaln_mod(c_pad, w_ada, b_ada)

    x2 = x.reshape(m, d)
    for l in range(depth):
        mod6 = mod[l, :b].reshape(b * N_MOD, 1, d)
        w_in_b = w_in[l, :, :main_cols].astype(BF16)
        w_fg_b = jnp.pad(w_in[l, :, main_cols:], ((0, 0), (0, HEAD_DIM - h_att))).astype(BF16)
        w_out_b = w_out[l].astype(BF16)
        w_up_b = w_up[l].astype(BF16)
        w_down_b = w_down[l].astype(BF16)

        h = _norm_mod(x2.reshape(b, s, d), norm_mix_w[l], mod6, 1, 0).reshape(m, d)
        proj = _matmul([h], w_in_b, mode="plain", out_dtype=F32, tm=1024, tn=1024)
        fg = _matmul([h], w_fg_b, mode="plain", out_dtype=F32, tm=1024, tn=HEAD_DIM)
        proj3 = proj.reshape(b, s, main_cols)

        o_rec = _hgrn2(proj3, lower_bounds, rec_norm_w[l], layer=l, n_heads=h_rec, rows=512)

        bias_row = jnp.pad(fg_bias[l].astype(F32), (0, HEAD_DIM - h_att)).reshape(1, HEAD_DIM)
        fc, fr = _fcum(fg.reshape(b, s, HEAD_DIM), bias_row)
        fr5 = fr[:, :, :h_att, :].transpose(0, 2, 1, 3).reshape(b, h_att, s // fox_blk, 1, fox_blk)
        o_att = _fox(proj3, fc, fr5, q_norm_w[l], k_norm_w[l], n_heads=h_att,
                     q_blk0=4 * h_rec, k_blk0=4 * h_rec + h_att, v_blk0=4 * h_rec + 2 * h_att,
                     blk=fox_blk)

        x2 = _matmul([o_rec.reshape(m, -1), o_att.reshape(m, -1)], w_out_b, mode="resid",
                     out_dtype=F32, tm=1024, tn=1024, resid=(x2, mod6, 2), rows_per_batch=s)

        h2 = _norm_mod(x2.reshape(b, s, d), norm_ffn_w[l], mod6, 4, 3).reshape(m, d)
        u = _matmul([h2], w_up_b, mode="relu2", out_dtype=BF16, tm=1024, tn=1024)
        x2 = _matmul([u], w_down_b, mode="resid", out_dtype=F32, tm=1024, tn=1024, tk=2048,
                     resid=(x2, mod6, 5), rows_per_batch=s)
    return x2.reshape(b, s, d)
```

```python
import functools

import jax
import jax.numpy as jnp
from jax import lax
from jax.experimental import pallas as pl
from jax.experimental.pallas import tpu as pltpu

HEAD_DIM = 128
CHUNK = 64
SUB = 16
N_SUB = CHUNK // SUB
N_MOD = 6
EPS = 1e-6
MASK_VALUE = -1e30
K_MAX = 1.0 - 1e-6
LOG2E = 1.4426950408889634
VMEM_LIMIT_BYTES = 56 * 1024 * 1024

F32 = jnp.float32
BF16 = jnp.bfloat16

_NT = (((1,), (1,)), ((), ()))
_TN = (((0,), (0,)), ((), ()))


def _params(*sem):
    return pltpu.CompilerParams(dimension_semantics=sem, vmem_limit_bytes=VMEM_LIMIT_BYTES)


def _fit_tile(tile, dim):
    tile = min(tile, dim)
    while dim % tile:
        tile -= HEAD_DIM
    return tile


def _sigmoid(x):
    return 1.0 / (1.0 + jnp.exp(-x))


def _tril_cumsum(tril, x):
    hi = x.astype(BF16)
    r1 = x - hi.astype(F32)
    mid = r1.astype(BF16)
    lo = (r1 - mid.astype(F32)).astype(BF16)
    out = jnp.dot(tril, hi, preferred_element_type=F32)
    out = out + jnp.dot(tril, mid, preferred_element_type=F32)
    return out + jnp.dot(tril, lo, preferred_element_type=F32)


def _tril(n):
    r = lax.broadcasted_iota(jnp.int32, (n, n), 0)
    c = lax.broadcasted_iota(jnp.int32, (n, n), 1)
    return (r >= c).astype(BF16)


def _mod_kernel(c_ref, w_ref, b_ref, o_ref):
    c = c_ref[...]
    ca = (c * _sigmoid(c)).astype(BF16)
    w = w_ref[...].astype(BF16)
    o_ref[...] = jnp.dot(ca, w, preferred_element_type=F32) + b_ref[...]


def _adaln_mod(c_pad, w_ada, b_ada):
    depth, d, n = w_ada.shape
    rows = c_pad.shape[0]
    tn = min(512, n)
    return pl.pallas_call(
        _mod_kernel,
        grid=(depth, n // tn),
        in_specs=[
            pl.BlockSpec((rows, d), lambda l, j: (0, 0)),
            pl.BlockSpec((None, d, tn), lambda l, j: (l, 0, j)),
            pl.BlockSpec((None, 1, tn), lambda l, j: (l, 0, j)),
        ],
        out_specs=pl.BlockSpec((None, rows, tn), lambda l, j: (l, 0, j)),
        out_shape=jax.ShapeDtypeStruct((depth, rows, n), F32),
        compiler_params=_params("arbitrary", "arbitrary"),
        name="adaln_mod",
    )(c_pad, w_ada, b_ada.reshape(depth, 1, n))


def _norm_mod_kernel(x_ref, w_ref, scale_ref, shift_ref, o_ref):
    x = x_ref[...]
    ms = jnp.mean(x * x, axis=-1, keepdims=True)
    y = x * lax.rsqrt(ms + EPS) * w_ref[...]
    o_ref[...] = (y * (1.0 + scale_ref[...]) + shift_ref[...]).astype(o_ref.dtype)


def _norm_mod(x3, w, mod6, scale_idx, shift_idx):
    b, s, d = x3.shape
    ts = min(512, s)
    return pl.pallas_call(
        _norm_mod_kernel,
        grid=(b, s // ts),
        in_specs=[
            pl.BlockSpec((None, ts, d), lambda i, t: (i, t, 0)),
            pl.BlockSpec((1, d), lambda i, t: (0, 0)),
            pl.BlockSpec((None, 1, d), lambda i, t: (i * N_MOD + scale_idx, 0, 0)),
            pl.BlockSpec((None, 1, d), lambda i, t: (i * N_MOD + shift_idx, 0, 0)),
        ],
        out_specs=pl.BlockSpec((None, ts, d), lambda i, t: (i, t, 0)),
        out_shape=jax.ShapeDtypeStruct((b, s, d), BF16),
        compiler_params=_params("arbitrary", "arbitrary"),
        name="norm_mod",
    )(x3, w.reshape(1, d), mod6, mod6)


def _mm_kernel(*refs, n_a, nk, mode):
    a_refs = refs[:n_a]
    w_ref = refs[n_a]
    rest = refs[n_a + 1:]
    if mode == "resid":
        x_ref, g_ref, o_ref = rest[:3]
        rest = rest[3:]
    else:
        o_ref = rest[0]
        rest = rest[1:]

    part = None
    off = 0
    for a_ref in a_refs:
        ka = a_ref.shape[1]
        p = jnp.dot(a_ref[...], w_ref[off:off + ka, :], preferred_element_type=F32)
        part = p if part is None else part + p
        off += ka

    def finish(acc):
        if mode == "plain":
            o_ref[...] = acc.astype(o_ref.dtype)
        elif mode == "relu2":
            r = jnp.maximum(acc, 0.0)
            o_ref[...] = (r * r).astype(o_ref.dtype)
        else:
            o_ref[...] = x_ref[...] + g_ref[...] * acc

    if nk == 1:
        finish(part)
    else:
        acc_ref = rest[0]
        k = pl.program_id(2)

        @pl.when(k == 0)
        def _():
            acc_ref[...] = part

        @pl.when(jnp.logical_and(k > 0, k < nk - 1))
        def _():
            acc_ref[...] += part

        @pl.when(k == nk - 1)
        def _():
            finish(acc_ref[...] + part)


def _matmul(a_list, w, *, mode, out_dtype, tm, tn, tk=None, resid=None, rows_per_batch=None):
    m = a_list[0].shape[0]
    kdim, n = w.shape
    n_a = len(a_list)
    tm = _fit_tile(tm, m)
    tn = _fit_tile(tn, n)
    if tk is None or tk >= kdim:
        tk = kdim
    nk = kdim // tk
    assert n_a == 1 or nk == 1
    assert m % tm == 0 and n % tn == 0 and kdim % tk == 0

    in_specs = []
    if n_a == 1:
        in_specs.append(pl.BlockSpec((tm, tk), lambda i, j, k: (i, k)))
    else:
        for a in a_list:
            in_specs.append(pl.BlockSpec((tm, a.shape[1]), lambda i, j, k: (i, 0)))
    in_specs.append(pl.BlockSpec((tk, tn), lambda i, j, k: (k, j)))
    args = list(a_list) + [w]
    if mode == "resid":
        x2, mod6, gate_idx = resid
        assert rows_per_batch % tm == 0
        tiles_per_batch = rows_per_batch // tm
        in_specs.append(pl.BlockSpec((tm, tn), lambda i, j, k: (i, j)))
        in_specs.append(pl.BlockSpec(
            (None, 1, tn), lambda i, j, k: ((i // tiles_per_batch) * N_MOD + gate_idx, 0, j)))
        args += [x2, mod6]
    scratch = [pltpu.VMEM((tm, tn), F32)] if nk > 1 else []
    return pl.pallas_call(
        functools.partial(_mm_kernel, n_a=n_a, nk=nk, mode=mode),
        grid=(m // tm, n // tn, nk),
        in_specs=in_specs,
        out_specs=pl.BlockSpec((tm, tn), lambda i, j, k: (i, j)),
        out_shape=jax.ShapeDtypeStruct((m, n), out_dtype),
        scratch_shapes=scratch,
        compiler_params=_params("arbitrary", "arbitrary", "arbitrary"),
        name="mm_" + mode,
    )(*args)


def _fcum_kernel(fg_ref, bias_ref, fc_ref):
    n_slab = fg_ref.shape[0] // HEAD_DIM
    tril = _tril(HEAD_DIM)

    def body(r, carry):
        rows = pl.ds(pl.multiple_of(r * HEAD_DIM, HEAD_DIM), HEAD_DIM)
        z = fg_ref[rows, :] + bias_ref[...]
        lf = jnp.minimum(z, 0.0) - jnp.log1p(jnp.exp(-jnp.abs(z)))
        f = _tril_cumsum(tril, lf) + carry
        fc_ref[rows, :] = f
        return f[HEAD_DIM - 1:HEAD_DIM, :]

    lax.fori_loop(0, n_slab, body, jnp.zeros((1, HEAD_DIM), F32))


def _fcum(fg3, bias_row):
    b, s, _ = fg3.shape
    return pl.pallas_call(
        _fcum_kernel,
        grid=(b,),
        in_specs=[
            pl.BlockSpec((None, s, HEAD_DIM), lambda i: (i, 0, 0)),
            pl.BlockSpec((1, HEAD_DIM), lambda i: (0, 0)),
        ],
        out_specs=pl.BlockSpec((None, s, HEAD_DIM), lambda i: (i, 0, 0)),
        out_shape=jax.ShapeDtypeStruct((b, s, HEAD_DIM), F32),
        compiler_params=_params("arbitrary"),
        name="fox_fcum",
    )(fg3, bias_row)


V_ROWS = HEAD_DIM + 16


def _fox_kernel(q_ref, k_ref, v_ref, fc_ref, qn_ref, kn_ref, o_ref,
                kaug_s, vaug_s, qaug_s, s0_s, s1_s, m_s, acc_s, *, blk, n_blk):
    h = pl.program_id(1)
    qi = pl.program_id(2)
    lane = lax.broadcasted_iota(jnp.int32, (blk, HEAD_DIM), 1)

    @pl.when(qi == 0)
    def _():
        ones_row = (lax.broadcasted_iota(jnp.int32, (V_ROWS - HEAD_DIM, blk), 0) == 0)
        for c in range(n_blk):
            rows = slice(c * blk, (c + 1) * blk)
            k = k_ref[rows, :]
            ms = jnp.mean(k * k, axis=-1, keepdims=True)
            kaug_s[rows, 0:HEAD_DIM] = (k * lax.rsqrt(ms + EPS) * kn_ref[...]).astype(BF16)
            fcol = jnp.sum(jnp.where(lane == h, fc_ref[rows, :], 0.0), axis=1, keepdims=True)
            nf = fcol * (-LOG2E)
            hi = nf.astype(BF16).astype(F32)
            mid = (nf - hi).astype(BF16).astype(F32)
            lo = (nf - hi) - mid
            aug = jnp.where(lane == 0, hi, jnp.where(lane == 1, mid, jnp.where(lane == 2, lo, 0.0)))
            kaug_s[rows, HEAD_DIM:2 * HEAD_DIM] = aug.astype(BF16)
            vaug_s[c, 0:HEAD_DIM, :] = v_ref[rows, :].T.astype(BF16)
            vaug_s[c, HEAD_DIM:V_ROWS, :] = ones_row.astype(BF16)

    q = q_ref[...]
    ms = jnp.mean(q * q, axis=-1, keepdims=True)
    qaug_s[:, 0:HEAD_DIM] = (q * lax.rsqrt(ms + EPS) * qn_ref[...]
                             * (HEAD_DIM ** -0.5 * LOG2E)).astype(BF16)
    qaug_s[:, HEAD_DIM:2 * HEAD_DIM] = (lane < 3).astype(BF16)
    m_s[...] = jnp.full(m_s.shape, MASK_VALUE, F32)
    acc_s[...] = jnp.zeros(acc_s.shape, F32)

    def scores(kj):
        rows = pl.ds(pl.multiple_of(kj * blk, blk), blk)
        return lax.dot_general(kaug_s[rows, :], qaug_s[...], _NT, preferred_element_type=F32)

    def update(s_ref, kj, causal):
        st = s_ref[...]
        if causal:
            kr = lax.broadcasted_iota(jnp.int32, (blk, blk), 0)
            qc = lax.broadcasted_iota(jnp.int32, (blk, blk), 1)
            st = jnp.where(kr <= qc, st, MASK_VALUE)
        m_old = m_s[...]
        m_new = jnp.maximum(m_old, jnp.max(st, axis=0, keepdims=True))
        pt = jnp.exp2(st - m_new).astype(BF16)
        alpha = jnp.exp2(m_old - m_new)
        acc_s[...] = alpha * acc_s[...] + jnp.dot(vaug_s[kj], pt, preferred_element_type=F32)
        m_s[...] = m_new

    s0_s[...] = scores(0)

    def pair(i, carry):
        kj = 2 * i
        s1_s[...] = scores(kj + 1)
        update(s0_s, kj, False)
        s0_s[...] = scores(kj + 2)
        update(s1_s, kj + 1, False)
        return carry

    lax.fori_loop(0, qi // 2, pair, 0)

    @pl.when(qi % 2 == 0)
    def _():
        update(s0_s, qi, True)

    @pl.when(qi % 2 == 1)
    def _():
        s1_s[...] = scores(qi)
        update(s0_s, qi - 1, False)
        update(s1_s, qi, True)

    acc = acc_s[...]
    ot = acc[0:HEAD_DIM, :] / acc[HEAD_DIM:HEAD_DIM + 1, :]
    o_ref[...] = ot.T.astype(o_ref.dtype)


def _fox(proj3, fc, qn_w, kn_w, *, n_heads, q_blk0, k_blk0, v_blk0, blk):
    b, s, _ = proj3.shape
    n_blk = s // blk
    return pl.pallas_call(
        functools.partial(_fox_kernel, blk=blk, n_blk=n_blk),
        grid=(b, n_heads, n_blk),
        in_specs=[
            pl.BlockSpec((None, blk, HEAD_DIM), lambda i, h, t: (i, t, q_blk0 + h)),
            pl.BlockSpec((None, s, HEAD_DIM), lambda i, h, t: (i, 0, k_blk0 + h)),
            pl.BlockSpec((None, s, HEAD_DIM), lambda i, h, t: (i, 0, v_blk0 + h)),
            pl.BlockSpec((None, s, HEAD_DIM), lambda i, h, t: (i, 0, 0)),
            pl.BlockSpec((1, HEAD_DIM), lambda i, h, t: (0, 0)),
            pl.BlockSpec((1, HEAD_DIM), lambda i, h, t: (0, 0)),
        ],
        out_specs=pl.BlockSpec((None, blk, HEAD_DIM), lambda i, h, t: (i, t, h)),
        out_shape=jax.ShapeDtypeStruct((b, s, n_heads * HEAD_DIM), BF16),
        scratch_shapes=[
            pltpu.VMEM((s, 2 * HEAD_DIM), BF16),
            pltpu.VMEM((n_blk, V_ROWS, blk), BF16),
            pltpu.VMEM((blk, 2 * HEAD_DIM), BF16),
            pltpu.VMEM((blk, blk), F32),
            pltpu.VMEM((blk, blk), F32),
            pltpu.VMEM((1, blk), F32),
            pltpu.VMEM((V_ROWS, blk), F32),
        ],
        compiler_params=_params("arbitrary", "arbitrary", "arbitrary"),
        name="fox_attn",
    )(proj3, proj3, proj3, fc, qn_w.reshape(1, HEAD_DIM), kn_w.reshape(1, HEAD_DIM))


def _hgrn2_kernel(q_ref, f_ref, i_ref, g_ref, lb_ref, nw_ref, seg_ref, o_ref,
                  st_s, y_s, g_s, kk_s, qf_s, r_s, *, layer, rows):
    t_id = pl.program_id(2)
    n_chunk = rows // CHUNK

    @pl.when(t_id == 0)
    def _():
        st_s[...] = jnp.zeros(st_s.shape, F32)

    depth = lb_ref.shape[0]
    lbs = [lb_ref[j] for j in range(depth)]
    mx = lbs[0]
    for j in range(1, depth):
        mx = jnp.maximum(mx, lbs[j])
    es = [jnp.exp(v - mx) for v in lbs]
    den = es[0]
    for j in range(1, depth):
        den = den + es[j]
    lb = jnp.zeros((1, HEAD_DIM), F32)
    for j in range(1, layer + 1):
        lb = lb + es[j] / den

    tril = _tril(CHUNK)

    half = SUB // 2
    zeros_half = jnp.zeros((half, HEAD_DIM), F32)

    def prep(c, carry):
        rws = pl.ds(pl.multiple_of(c * CHUNK, CHUNK), CHUNK)
        z = f_ref[rws, :]
        kk = (1.0 - lb) * _sigmoid(-z)
        logf = jnp.log1p(-jnp.minimum(kk, K_MAX))
        gc = _tril_cumsum(tril, logf) * LOG2E
        q = q_ref[rws, :]
        qf = q * _sigmoid(q)
        kk_s[rws, :] = kk
        g_s[rws, :] = gc
        qf_s[rws, :] = qf
        for b in range(N_SUB):
            qb = qf[b * SUB:(b + 1) * SUB]
            kb = kk[b * SUB:(b + 1) * SUB]
            gb = gc[b * SUB:(b + 1) * SUB]
            yrows = pl.ds(pl.multiple_of(c * CHUNK + b * SUB, SUB), SUB)
            for s in range(SUB):
                lo = half if s >= half else 0
                e = jnp.exp2(jnp.minimum(gb[lo:] - gb[s:s + 1], 0.0))
                x = qb[lo:] * kb[s:s + 1] * e
                if lo:
                    x = jnp.concatenate([zeros_half, x], axis=0)
                y_s[yrows, s * HEAD_DIM:(s + 1) * HEAD_DIM] = x.astype(BF16)
        return carry

    lax.fori_loop(0, n_chunk, prep, 0, unroll=True)

    r_s[...] = jnp.dot(y_s[...], seg_ref[...], preferred_element_type=F32)

    tr = lax.broadcasted_iota(jnp.int32, (CHUNK, CHUNK), 0)
    tc = lax.broadcasted_iota(jnp.int32, (CHUNK, CHUNK), 1)
    same_blk = (tr // SUB) == (tc // SUB)
    diag_mask = jnp.logical_and(same_blk, tc <= tr)
    off_mask = (tc // SUB) < (tr // SUB)

    def chunk(c, carry):
        rws = pl.ds(pl.multiple_of(c * CHUNK, CHUNK), CHUNK)
        qf = qf_s[rws, :]
        kk = kk_s[rws, :]
        gc = g_s[rws, :]
        vb = i_ref[rws, :].astype(BF16)
        blocks = [jnp.zeros((SUB, CHUNK), F32)]
        for i in range(1, N_SUB):
            gn = gc[i * SUB:i * SUB + 1]
            ql = qf[i * SUB:(i + 1) * SUB] * jnp.exp2(gc[i * SUB:(i + 1) * SUB] - gn)
            kr = kk * jnp.exp2(jnp.minimum(gn - gc, 0.0))
            blocks.append(lax.dot_general(ql.astype(BF16), kr.astype(BF16), _NT,
                                          preferred_element_type=F32))
        a_off = jnp.concatenate(blocks, axis=0)
        a = jnp.where(diag_mask, r_s[rws, :][:, :CHUNK], jnp.where(off_mask, a_off, 0.0))
        st = st_s[...]
        qe = qf * jnp.exp2(gc)
        o = jnp.dot(a.astype(BF16), vb, preferred_element_type=F32)
        o = o + lax.dot_general(qe.astype(BF16), st.astype(BF16), _NT,
                                preferred_element_type=F32)
        g_last = gc[CHUNK - 1:CHUNK]
        kd = kk * jnp.exp2(g_last - gc)
        st_s[...] = jnp.exp2(g_last) * st + lax.dot_general(
            vb, kd.astype(BF16), _TN, preferred_element_type=F32)
        ms = jnp.mean(o * o, axis=-1, keepdims=True)
        y = o * lax.rsqrt(ms + EPS) * nw_ref[...]
        g = g_ref[rws, :]
        o_ref[rws, :] = (y * (g * _sigmoid(g))).astype(o_ref.dtype)
        return carry

    lax.fori_loop(0, n_chunk, chunk, 0, unroll=4)


def _hgrn2(proj3, lower_bounds, nw, *, layer, n_heads, rows):
    b, s, _ = proj3.shape
    depth = lower_bounds.shape[0]
    rows = min(rows, s)
    seg_r = jnp.arange(SUB * HEAD_DIM, dtype=jnp.int32)[:, None] // HEAD_DIM
    seg_c = jnp.arange(HEAD_DIM, dtype=jnp.int32)[None, :]
    seg = jnp.logical_and(seg_c < CHUNK, (seg_c % SUB) == seg_r).astype(BF16)
    lb4 = lower_bounds.astype(F32).reshape(depth, n_heads, 1, HEAD_DIM)

    def col(off):
        return pl.BlockSpec((None, rows, HEAD_DIM), lambda i, h, t: (i, t, off * n_heads + h))

    return pl.pallas_call(
        functools.partial(_hgrn2_kernel, layer=layer, rows=rows),
        grid=(b, n_heads, s // rows),
        in_specs=[
            col(0), col(1), col(2), col(3),
            pl.BlockSpec((depth, None, 1, HEAD_DIM), lambda i, h, t: (0, h, 0, 0)),
            pl.BlockSpec((1, HEAD_DIM), lambda i, h, t: (0, 0)),
            pl.BlockSpec((SUB * HEAD_DIM, HEAD_DIM), lambda i, h, t: (0, 0)),
        ],
        out_specs=pl.BlockSpec((None, rows, HEAD_DIM), lambda i, h, t: (i, t, h)),
        out_shape=jax.ShapeDtypeStruct((b, s, n_heads * HEAD_DIM), BF16),
        scratch_shapes=[
            pltpu.VMEM((HEAD_DIM, HEAD_DIM), F32),
            pltpu.VMEM((rows, SUB * HEAD_DIM), BF16),
            pltpu.VMEM((rows, HEAD_DIM), F32),
            pltpu.VMEM((rows, HEAD_DIM), F32),
            pltpu.VMEM((rows, HEAD_DIM), F32),
            pltpu.VMEM((rows, HEAD_DIM), F32),
        ],
        compiler_params=_params("arbitrary", "arbitrary", "arbitrary"),
        name="hgrn2",
    )(proj3, proj3, proj3, proj3, lb4, nw.reshape(1, HEAD_DIM), seg)


def kernel(x, c, lower_bounds, w_ada, b_ada, norm_mix_w, norm_ffn_w, w_in, rec_norm_w,
           fg_bias, q_norm_w, k_norm_w, w_out, w_up, w_down):
    b, s, d = x.shape
    depth = w_ada.shape[0]
    n_heads = d // HEAD_DIM
    h_rec = n_heads // 2
    h_att = n_heads - h_rec
    main_cols = (4 * h_rec + 3 * h_att) * HEAD_DIM
    assert w_in.shape[2] == main_cols + h_att and h_att <= HEAD_DIM
    assert s % CHUNK == 0 and s % HEAD_DIM == 0
    m = b * s
    fox_blk = min(512, s)

    rows_pad = -(-b // 8) * 8
    c_pad = jnp.zeros((rows_pad, d), F32).at[:b].set(c.astype(F32))
    mod = _adaln_mod(c_pad, w_ada, b_ada)

    x2 = x.reshape(m, d)
    for l in range(depth):
        mod6 = mod[l, :b].reshape(b * N_MOD, 1, d)
        w_in_b = w_in[l, :, :main_cols].astype(BF16)
        w_fg_b = jnp.pad(w_in[l, :, main_cols:], ((0, 0), (0, HEAD_DIM - h_att))).astype(BF16)
        w_out_b = w_out[l].astype(BF16)
        w_up_b = w_up[l].astype(BF16)
        w_down_b = w_down[l].astype(BF16)

        h = _norm_mod(x2.reshape(b, s, d), norm_mix_w[l], mod6, 1, 0).reshape(m, d)
        proj = _matmul([h], w_in_b, mode="plain", out_dtype=F32, tm=1024, tn=1024)
        fg = _matmul([h], w_fg_b, mode="plain", out_dtype=F32, tm=1024, tn=HEAD_DIM)
        proj3 = proj.reshape(b, s, main_cols)

        o_rec = _hgrn2(proj3, lower_bounds, rec_norm_w[l], layer=l, n_heads=h_rec, rows=512)

        bias_row = jnp.pad(fg_bias[l].astype(F32), (0, HEAD_DIM - h_att)).reshape(1, HEAD_DIM)
        fc = _fcum(fg.reshape(b, s, HEAD_DIM), bias_row)
        o_att = _fox(proj3, fc, q_norm_w[l], k_norm_w[l], n_heads=h_att,
                     q_blk0=4 * h_rec, k_blk0=4 * h_rec + h_att, v_blk0=4 * h_rec + 2 * h_att,
                     blk=fox_blk)

        x2 = _matmul([o_rec.reshape(m, -1), o_att.reshape(m, -1)], w_out_b, mode="resid",
                     out_dtype=F32, tm=1024, tn=1024, resid=(x2, mod6, 2), rows_per_batch=s)

        h2 = _norm_mod(x2.reshape(b, s, d), norm_ffn_w[l], mod6, 4, 3).reshape(m, d)
        u = _matmul([h2], w_up_b, mode="relu2", out_dtype=BF16, tm=1024, tn=1024)
        x2 = _matmul([u], w_down_b, mode="resid", out_dtype=F32, tm=1024, tn=1024, tk=2048,
                     resid=(x2, mod6, 5), rows_per_batch=s)
    return x2.reshape(b, s, d)
```

```python
import functools

import jax
import jax.numpy as jnp
from jax import lax
from jax.experimental import pallas as pl
from jax.experimental.pallas import tpu as pltpu

HEAD_DIM = 128
CHUNK = 64
SUB = 16
N_SUB = CHUNK // SUB
N_MOD = 6
EPS = 1e-6
MASK_VALUE = -1e30
K_MAX = 1.0 - 1e-6
LOG2E = 1.4426950408889634
VMEM_LIMIT_BYTES = 60 * 1024 * 1024

F32 = jnp.float32
BF16 = jnp.bfloat16

_NT = (((1,), (1,)), ((), ()))
_TN = (((0,), (0,)), ((), ()))


def _params(*sem):
    return pltpu.CompilerParams(dimension_semantics=sem, vmem_limit_bytes=VMEM_LIMIT_BYTES)


def _fit_tile(tile, dim):
    tile = min(tile, dim)
    while dim % tile:
        tile -= HEAD_DIM
    return tile


def _sigmoid(x):
    return 1.0 / (1.0 + jnp.exp(-x))


def _tril_cumsum(tril, x):
    hi = x.astype(BF16)
    r1 = x - hi.astype(F32)
    mid = r1.astype(BF16)
    lo = (r1 - mid.astype(F32)).astype(BF16)
    out = jnp.dot(tril, hi, preferred_element_type=F32)
    out = out + jnp.dot(tril, mid, preferred_element_type=F32)
    return out + jnp.dot(tril, lo, preferred_element_type=F32)


def _tril(n):
    r = lax.broadcasted_iota(jnp.int32, (n, n), 0)
    c = lax.broadcasted_iota(jnp.int32, (n, n), 1)
    return (r >= c).astype(BF16)


def _mod_kernel(c_ref, w_ref, b_ref, o_ref):
    c = c_ref[...]
    ca = (c * _sigmoid(c)).astype(BF16)
    w = w_ref[...].astype(BF16)
    o_ref[...] = jnp.dot(ca, w, preferred_element_type=F32) + b_ref[...]


def _adaln_mod(c_pad, w_ada, b_ada):
    depth, d, n = w_ada.shape
    rows = c_pad.shape[0]
    tn = min(512, n)
    return pl.pallas_call(
        _mod_kernel,
        grid=(depth, n // tn),
        in_specs=[
            pl.BlockSpec((rows, d), lambda l, j: (0, 0)),
            pl.BlockSpec((None, d, tn), lambda l, j: (l, 0, j)),
            pl.BlockSpec((None, 1, tn), lambda l, j: (l, 0, j)),
        ],
        out_specs=pl.BlockSpec((None, rows, tn), lambda l, j: (l, 0, j)),
        out_shape=jax.ShapeDtypeStruct((depth, rows, n), F32),
        compiler_params=_params("arbitrary", "arbitrary"),
        name="adaln_mod",
    )(c_pad, w_ada, b_ada.reshape(depth, 1, n))


def _norm_mod_kernel(x_ref, w_ref, scale_ref, shift_ref, o_ref):
    x = x_ref[...]
    ms = jnp.mean(x * x, axis=-1, keepdims=True)
    y = x * lax.rsqrt(ms + EPS) * w_ref[...]
    o_ref[...] = (y * (1.0 + scale_ref[...]) + shift_ref[...]).astype(o_ref.dtype)


def _norm_mod(x3, w, mod6, scale_idx, shift_idx):
    b, s, d = x3.shape
    ts = min(512, s)
    return pl.pallas_call(
        _norm_mod_kernel,
        grid=(b, s // ts),
        in_specs=[
            pl.BlockSpec((None, ts, d), lambda i, t: (i, t, 0)),
            pl.BlockSpec((1, d), lambda i, t: (0, 0)),
            pl.BlockSpec((None, 1, d), lambda i, t: (i * N_MOD + scale_idx, 0, 0)),
            pl.BlockSpec((None, 1, d), lambda i, t: (i * N_MOD + shift_idx, 0, 0)),
        ],
        out_specs=pl.BlockSpec((None, ts, d), lambda i, t: (i, t, 0)),
        out_shape=jax.ShapeDtypeStruct((b, s, d), BF16),
        compiler_params=_params("arbitrary", "arbitrary"),
        name="norm_mod",
    )(x3, w.reshape(1, d), mod6, mod6)


def _mm_kernel(*refs, n_a, nk, mode):
    a_refs = refs[:n_a]
    w_ref = refs[n_a]
    rest = refs[n_a + 1:]
    if mode == "resid":
        x_ref, g_ref, o_ref = rest[:3]
        rest = rest[3:]
    else:
        o_ref = rest[0]
        rest = rest[1:]

    part = None
    off = 0
    for a_ref in a_refs:
        ka = a_ref.shape[1]
        p = jnp.dot(a_ref[...], w_ref[off:off + ka, :], preferred_element_type=F32)
        part = p if part is None else part + p
        off += ka

    def finish(acc):
        if mode == "plain":
            o_ref[...] = acc.astype(o_ref.dtype)
        elif mode == "relu2":
            r = jnp.maximum(acc, 0.0)
            o_ref[...] = (r * r).astype(o_ref.dtype)
        else:
            o_ref[...] = x_ref[...] + g_ref[...] * acc

    if nk == 1:
        finish(part)
    else:
        k = pl.program_id(2)

        @pl.when(k == 0)
        def _():
            o_ref[...] = part

        @pl.when(jnp.logical_and(k > 0, k < nk - 1))
        def _():
            o_ref[...] += part

        @pl.when(k == nk - 1)
        def _():
            finish(o_ref[...] + part)


def _matmul(a_list, w_stack, layer, *, mode, out_dtype, tm, tn, tk=None, n_cols=None,
            resid=None, rows_per_batch=None):
    m = a_list[0].shape[0]
    _, kdim, n_total = w_stack.shape
    n = n_total if n_cols is None else n_cols
    n_a = len(a_list)
    tm = _fit_tile(tm, m)
    tn = _fit_tile(tn, n)
    if tk is None or tk >= kdim:
        tk = kdim
    nk = kdim // tk
    assert n_a == 1 or nk == 1
    assert nk == 1 or out_dtype == F32
    assert m % tm == 0 and n % tn == 0 and kdim % tk == 0

    in_specs = []
    if n_a == 1:
        in_specs.append(pl.BlockSpec((tm, tk), lambda i, j, k: (i, k)))
    else:
        for a in a_list:
            in_specs.append(pl.BlockSpec((tm, a.shape[1]), lambda i, j, k: (i, 0)))
    in_specs.append(pl.BlockSpec((None, tk, tn), lambda i, j, k: (layer, k, j)))
    args = list(a_list) + [w_stack]
    if mode == "resid":
        x2, mod6, gate_idx = resid
        assert rows_per_batch % tm == 0
        tiles_per_batch = rows_per_batch // tm
        in_specs.append(pl.BlockSpec((tm, tn), lambda i, j, k: (i, j)))
        in_specs.append(pl.BlockSpec(
            (None, 1, tn), lambda i, j, k: ((i // tiles_per_batch) * N_MOD + gate_idx, 0, j)))
        args += [x2, mod6]
    return pl.pallas_call(
        functools.partial(_mm_kernel, n_a=n_a, nk=nk, mode=mode),
        grid=(m // tm, n // tn, nk),
        in_specs=in_specs,
        out_specs=pl.BlockSpec((tm, tn), lambda i, j, k: (i, j)),
        out_shape=jax.ShapeDtypeStruct((m, n), out_dtype),
        compiler_params=_params("arbitrary", "arbitrary", "arbitrary"),
        name="mm_" + mode,
    )(*args)


def _fcum_kernel(fg_ref, bias_ref, fc_ref):
    n_slab = fg_ref.shape[0] // HEAD_DIM
    tril = _tril(HEAD_DIM)

    def body(r, carry):
        rows = pl.ds(pl.multiple_of(r * HEAD_DIM, HEAD_DIM), HEAD_DIM)
        z = fg_ref[rows, :] + bias_ref[...]
        lf = jnp.minimum(z, 0.0) - jnp.log1p(jnp.exp(-jnp.abs(z)))
        f = _tril_cumsum(tril, lf) + carry
        fc_ref[rows, :] = f
        return f[HEAD_DIM - 1:HEAD_DIM, :]

    lax.fori_loop(0, n_slab, body, jnp.zeros((1, HEAD_DIM), F32))


def _fcum(fg3, bias_row):
    b, s, _ = fg3.shape
    return pl.pallas_call(
        _fcum_kernel,
        grid=(b,),
        in_specs=[
            pl.BlockSpec((None, s, HEAD_DIM), lambda i: (i, 0, 0)),
            pl.BlockSpec((1, HEAD_DIM), lambda i: (0, 0)),
        ],
        out_specs=pl.BlockSpec((None, s, HEAD_DIM), lambda i: (i, 0, 0)),
        out_shape=jax.ShapeDtypeStruct((b, s, HEAD_DIM), F32),
        compiler_params=_params("arbitrary"),
        name="fox_fcum",
    )(fg3, bias_row)


V_ROWS = HEAD_DIM + 16


def _fox_kernel(q_ref, k_ref, v_ref, fc_ref, qn_ref, kn_ref, o_ref,
                kaug_s, vaug_s, qaug_s, s0_s, s1_s, m_s, acc_s, *, blk, n_blk):
    h = pl.program_id(1)
    qi = pl.program_id(2)
    lane = lax.broadcasted_iota(jnp.int32, (blk, HEAD_DIM), 1)

    @pl.when(qi == 0)
    def _():
        ones_row = (lax.broadcasted_iota(jnp.int32, (V_ROWS - HEAD_DIM, blk), 0) == 0)
        for c in range(n_blk):
            rows = slice(c * blk, (c + 1) * blk)
            k = k_ref[rows, :]
            ms = jnp.mean(k * k, axis=-1, keepdims=True)
            kaug_s[rows, 0:HEAD_DIM] = (k * lax.rsqrt(ms + EPS) * kn_ref[...]).astype(BF16)
            fcol = jnp.sum(jnp.where(lane == h, fc_ref[rows, :], 0.0), axis=1, keepdims=True)
            nf = fcol * (-LOG2E)
            hi = nf.astype(BF16).astype(F32)
            mid = (nf - hi).astype(BF16).astype(F32)
            lo = (nf - hi) - mid
            aug = jnp.where(lane == 0, hi, jnp.where(lane == 1, mid, jnp.where(lane == 2, lo, 0.0)))
            kaug_s[rows, HEAD_DIM:2 * HEAD_DIM] = aug.astype(BF16)
            vaug_s[c, 0:HEAD_DIM, :] = v_ref[rows, :].T.astype(BF16)
            vaug_s[c, HEAD_DIM:V_ROWS, :] = ones_row.astype(BF16)

    q = q_ref[...]
    ms = jnp.mean(q * q, axis=-1, keepdims=True)
    qaug_s[:, 0:HEAD_DIM] = (q * lax.rsqrt(ms + EPS) * qn_ref[...]
                             * (HEAD_DIM ** -0.5 * LOG2E)).astype(BF16)
    qaug_s[:, HEAD_DIM:2 * HEAD_DIM] = (lane < 3).astype(BF16)
    m_s[...] = jnp.full(m_s.shape, MASK_VALUE, F32)
    acc_s[...] = jnp.zeros(acc_s.shape, F32)

    def scores(kj):
        rows = pl.ds(pl.multiple_of(kj * blk, blk), blk)
        return lax.dot_general(kaug_s[rows, :], qaug_s[...], _NT, preferred_element_type=F32)

    def update(s_ref, kj, causal):
        st = s_ref[...]
        if causal:
            kr = lax.broadcasted_iota(jnp.int32, (blk, blk), 0)
            qc = lax.broadcasted_iota(jnp.int32, (blk, blk), 1)
            st = jnp.where(kr <= qc, st, MASK_VALUE)
        m_old = m_s[...]
        m_new = jnp.maximum(m_old, jnp.max(st, axis=0, keepdims=True))
        pt = jnp.exp2(st - m_new).astype(BF16)
        alpha = jnp.exp2(m_old - m_new)
        acc_s[...] = alpha * acc_s[...] + jnp.dot(vaug_s[kj], pt, preferred_element_type=F32)
        m_s[...] = m_new

    s0_s[...] = scores(0)

    def pair(i, carry):
        kj = 2 * i
        s1_s[...] = scores(kj + 1)
        update(s0_s, kj, False)
        s0_s[...] = scores(kj + 2)
        update(s1_s, kj + 1, False)
        return carry

    lax.fori_loop(0, qi // 2, pair, 0)

    @pl.when(qi % 2 == 0)
    def _():
        update(s0_s, qi, True)

    @pl.when(qi % 2 == 1)
    def _():
        s1_s[...] = scores(qi)
        update(s0_s, qi - 1, False)
        update(s1_s, qi, True)

    acc = acc_s[...]
    ot = acc[0:HEAD_DIM, :] / acc[HEAD_DIM:HEAD_DIM + 1, :]
    o_ref[...] = ot.T.astype(o_ref.dtype)


def _fox(proj3, fc, qn_w, kn_w, *, n_heads, q_blk0, k_blk0, v_blk0, blk):
    b, s, _ = proj3.shape
    n_blk = s // blk
    return pl.pallas_call(
        functools.partial(_fox_kernel, blk=blk, n_blk=n_blk),
        grid=(b, n_heads, n_blk),
        in_specs=[
            pl.BlockSpec((None, blk, HEAD_DIM), lambda i, h, t: (i, t, q_blk0 + h)),
            pl.BlockSpec((None, s, HEAD_DIM), lambda i, h, t: (i, 0, k_blk0 + h)),
            pl.BlockSpec((None, s, HEAD_DIM), lambda i, h, t: (i, 0, v_blk0 + h)),
            pl.BlockSpec((None, s, HEAD_DIM), lambda i, h, t: (i, 0, 0)),
            pl.BlockSpec((1, HEAD_DIM), lambda i, h, t: (0, 0)),
            pl.BlockSpec((1, HEAD_DIM), lambda i, h, t: (0, 0)),
        ],
        out_specs=pl.BlockSpec((None, blk, HEAD_DIM), lambda i, h, t: (i, t, h)),
        out_shape=jax.ShapeDtypeStruct((b, s, n_heads * HEAD_DIM), BF16),
        scratch_shapes=[
            pltpu.VMEM((s, 2 * HEAD_DIM), BF16),
            pltpu.VMEM((n_blk, V_ROWS, blk), BF16),
            pltpu.VMEM((blk, 2 * HEAD_DIM), BF16),
            pltpu.VMEM((blk, blk), F32),
            pltpu.VMEM((blk, blk), F32),
            pltpu.VMEM((1, blk), F32),
            pltpu.VMEM((V_ROWS, blk), F32),
        ],
        compiler_params=_params("arbitrary", "arbitrary", "arbitrary"),
        name="fox_attn",
    )(proj3, proj3, proj3, fc, qn_w.reshape(1, HEAD_DIM), kn_w.reshape(1, HEAD_DIM))


def _hgrn2_kernel(q_ref, f_ref, i_ref, g_ref, lb_ref, nw_ref, seg_ref, o_ref,
                  st_s, y_s, g_s, kk_s, qf_s, r_s, a_s, qe_s, u_s, dec_s, *, layer, rows):
    t_id = pl.program_id(2)
    n_chunk = rows // CHUNK

    @pl.when(t_id == 0)
    def _():
        st_s[...] = jnp.zeros(st_s.shape, F32)

    depth = lb_ref.shape[0]
    lbs = [lb_ref[j] for j in range(depth)]
    mx = lbs[0]
    for j in range(1, depth):
        mx = jnp.maximum(mx, lbs[j])
    es = [jnp.exp(v - mx) for v in lbs]
    den = es[0]
    for j in range(1, depth):
        den = den + es[j]
    lb = jnp.zeros((1, HEAD_DIM), F32)
    for j in range(1, layer + 1):
        lb = lb + es[j] / den

    tril = _tril(CHUNK)

    half = SUB // 2
    zeros_half = jnp.zeros((half, HEAD_DIM), F32)
    row_half = lax.broadcasted_iota(jnp.int32, (half, HEAD_DIM), 0)

    def prep(c, carry):
        rws = pl.ds(pl.multiple_of(c * CHUNK, CHUNK), CHUNK)
        z = f_ref[rws, :]
        kk = (1.0 - lb) * _sigmoid(-z)
        logf = jnp.log1p(-jnp.minimum(kk, K_MAX))
        gc = _tril_cumsum(tril, logf) * LOG2E
        q = q_ref[rws, :]
        qf = q * _sigmoid(q)
        kk_s[rws, :] = kk
        g_s[rws, :] = gc
        qf_s[rws, :] = qf
        for b in range(N_SUB):
            r0 = b * SUB
            q_t, q_b = qf[r0:r0 + half], qf[r0 + half:r0 + SUB]
            k_t, k_b = kk[r0:r0 + half], kk[r0 + half:r0 + SUB]
            g_t, g_b = gc[r0:r0 + half], gc[r0 + half:r0 + SUB]
            m_t = g_t[half // 2:half // 2 + 1]
            m_b = g_b[half // 2:half // 2 + 1]
            g_8 = g_b[0:1]
            qt_f = q_t * jnp.exp2(g_t - m_t)
            pt_f = k_t * jnp.exp2(m_t - g_t)
            qb_f = q_b * jnp.exp2(g_b - m_b)
            pb_f = k_b * jnp.exp2(m_b - g_b)
            qx_f = q_b * jnp.exp2(g_b - g_8)
            px_f = k_t * jnp.exp2(g_8 - g_t)
            yrows = pl.ds(pl.multiple_of(c * CHUNK + r0, SUB), SUB)
            for s in range(SUB):
                if s < half:
                    top = jnp.where(row_half >= s, qt_f * pt_f[s:s + 1], 0.0)
                    bot = qx_f * px_f[s:s + 1]
                else:
                    top = zeros_half
                    bot = jnp.where(row_half >= s - half, qb_f * pb_f[s - half:s - half + 1], 0.0)
                x = jnp.concatenate([top, bot], axis=0)
                y_s[yrows, s * HEAD_DIM:(s + 1) * HEAD_DIM] = x.astype(BF16)
        return carry

    lax.fori_loop(0, n_chunk, prep, 0, unroll=True)

    r_s[...] = jnp.dot(y_s[...], seg_ref[...], preferred_element_type=F32)

    tr = lax.broadcasted_iota(jnp.int32, (CHUNK, CHUNK), 0)
    tc = lax.broadcasted_iota(jnp.int32, (CHUNK, CHUNK), 1)
    same_blk = (tr // SUB) == (tc // SUB)
    diag_mask = jnp.logical_and(same_blk, tc <= tr)
    off_mask = (tc // SUB) < (tr // SUB)

    def scores(c, carry):
        rws = pl.ds(pl.multiple_of(c * CHUNK, CHUNK), CHUNK)
        qf = qf_s[rws, :]
        kk = kk_s[rws, :]
        gc = g_s[rws, :]
        blocks = [jnp.zeros((SUB, CHUNK), F32)]
        for i in range(1, N_SUB):
            gn = gc[i * SUB:i * SUB + 1]
            ql = qf[i * SUB:(i + 1) * SUB] * jnp.exp2(gc[i * SUB:(i + 1) * SUB] - gn)
            kr = kk * jnp.exp2(jnp.minimum(gn - gc, 0.0))
            blocks.append(lax.dot_general(ql.astype(BF16), kr.astype(BF16), _NT,
                                          preferred_element_type=F32))
        a_off = jnp.concatenate(blocks, axis=0)
        a = jnp.where(diag_mask, r_s[rws, :][:, :CHUNK], jnp.where(off_mask, a_off, 0.0))
        a_s[rws, :] = a.astype(BF16)
        g_last = gc[CHUNK - 1:CHUNK]
        qe_s[rws, :] = (qf * jnp.exp2(gc)).astype(BF16)
        kd = kk * jnp.exp2(g_last - gc)
        u_s[c] = lax.dot_general(i_ref[rws, :].astype(BF16), kd.astype(BF16), _TN,
                                 preferred_element_type=F32)
        dec_s[c] = jnp.exp2(g_last)
        return carry

    lax.fori_loop(0, n_chunk, scores, 0, unroll=True)

    def chunk(c, carry):
        rws = pl.ds(pl.multiple_of(c * CHUNK, CHUNK), CHUNK)
        vb = i_ref[rws, :].astype(BF16)
        st = st_s[...]
        o = jnp.dot(a_s[rws, :], vb, preferred_element_type=F32)
        o = o + lax.dot_general(qe_s[rws, :], st.astype(BF16), _NT,
                                preferred_element_type=F32)
        st_s[...] = dec_s[c] * st + u_s[c]
        ms = jnp.mean(o * o, axis=-1, keepdims=True)
        y = o * lax.rsqrt(ms + EPS) * nw_ref[...]
        g = g_ref[rws, :]
        o_ref[rws, :] = (y * (g * _sigmoid(g))).astype(o_ref.dtype)
        return carry

    lax.fori_loop(0, n_chunk, chunk, 0, unroll=True)


def _hgrn2(proj3, lower_bounds, nw, *, layer, n_heads, rows):
    b, s, _ = proj3.shape
    depth = lower_bounds.shape[0]
    rows = min(rows, s)
    seg_r = jnp.arange(SUB * HEAD_DIM, dtype=jnp.int32)[:, None] // HEAD_DIM
    seg_c = jnp.arange(HEAD_DIM, dtype=jnp.int32)[None, :]
    seg = jnp.logical_and(seg_c < CHUNK, (seg_c % SUB) == seg_r).astype(BF16)
    lb4 = lower_bounds.astype(F32).reshape(depth, n_heads, 1, HEAD_DIM)

    def col(off):
        return pl.BlockSpec((None, rows, HEAD_DIM), lambda i, h, t: (i, t, off * n_heads + h))

    return pl.pallas_call(
        functools.partial(_hgrn2_kernel, layer=layer, rows=rows),
        grid=(b, n_heads, s // rows),
        in_specs=[
            col(0), col(1), col(2), col(3),
            pl.BlockSpec((depth, None, 1, HEAD_DIM), lambda i, h, t: (0, h, 0, 0)),
            pl.BlockSpec((1, HEAD_DIM), lambda i, h, t: (0, 0)),
            pl.BlockSpec((SUB * HEAD_DIM, HEAD_DIM), lambda i, h, t: (0, 0)),
        ],
        out_specs=pl.BlockSpec((None, rows, HEAD_DIM), lambda i, h, t: (i, t, h)),
        out_shape=jax.ShapeDtypeStruct((b, s, n_heads * HEAD_DIM), BF16),
        scratch_shapes=[
            pltpu.VMEM((HEAD_DIM, HEAD_DIM), F32),
            pltpu.VMEM((rows, SUB * HEAD_DIM), BF16),
            pltpu.VMEM((rows, HEAD_DIM), F32),
            pltpu.VMEM((rows, HEAD_DIM), F32),
            pltpu.VMEM((rows, HEAD_DIM), F32),
            pltpu.VMEM((rows, HEAD_DIM), F32),
            pltpu.VMEM((rows, CHUNK), BF16),
            pltpu.VMEM((rows, HEAD_DIM), BF16),
            pltpu.VMEM((rows // CHUNK, HEAD_DIM, HEAD_DIM), F32),
            pltpu.VMEM((rows // CHUNK, 1, HEAD_DIM), F32),
        ],
        compiler_params=_params("arbitrary", "arbitrary", "arbitrary"),
        name="hgrn2",
    )(proj3, proj3, proj3, proj3, lb4, nw.reshape(1, HEAD_DIM), seg)


def kernel(x, c, lower_bounds, w_ada, b_ada, norm_mix_w, norm_ffn_w, w_in, rec_norm_w,
           fg_bias, q_norm_w, k_norm_w, w_out, w_up, w_down):
    b, s, d = x.shape
    depth = w_ada.shape[0]
    n_heads = d // HEAD_DIM
    h_rec = n_heads // 2
    h_att = n_heads - h_rec
    main_cols = (4 * h_rec + 3 * h_att) * HEAD_DIM
    assert w_in.shape[2] == main_cols + h_att and h_att <= HEAD_DIM
    assert s % CHUNK == 0 and s % HEAD_DIM == 0
    m = b * s
    fox_blk = min(512, s)

    rows_pad = -(-b // 8) * 8
    c_pad = jnp.zeros((rows_pad, d), F32).at[:b].set(c.astype(F32))
    mod = _adaln_mod(c_pad, w_ada, b_ada)

    w_in_b = w_in.astype(BF16)
    w_fg_b = jnp.pad(w_in[:, :, main_cols:],
                     ((0, 0), (0, 0), (0, HEAD_DIM - h_att))).astype(BF16)
    w_out_b = w_out.astype(BF16)
    w_up_b = w_up.astype(BF16)
    w_down_b = w_down.astype(BF16)

    x2 = x.reshape(m, d)
    for l in range(depth):
        mod6 = mod[l, :b].reshape(b * N_MOD, 1, d)

        h = _norm_mod(x2.reshape(b, s, d), norm_mix_w[l], mod6, 1, 0).reshape(m, d)
        proj = _matmul([h], w_in_b, l, mode="plain", out_dtype=F32, tm=1024, tn=1024,
                       n_cols=main_cols)
        fg = _matmul([h], w_fg_b, l, mode="plain", out_dtype=F32, tm=1024, tn=HEAD_DIM)
        proj3 = proj.reshape(b, s, main_cols)

        o_rec = _hgrn2(proj3, lower_bounds, rec_norm_w[l], layer=l, n_heads=h_rec, rows=512)

        bias_row = jnp.pad(fg_bias[l].astype(F32), (0, HEAD_DIM - h_att)).reshape(1, HEAD_DIM)
        fc = _fcum(fg.reshape(b, s, HEAD_DIM), bias_row)
        o_att = _fox(proj3, fc, q_norm_w[l], k_norm_w[l], n_heads=h_att,
                     q_blk0=4 * h_rec, k_blk0=4 * h_rec + h_att, v_blk0=4 * h_rec + 2 * h_att,
                     blk=fox_blk)

        x2 = _matmul([o_rec.reshape(m, -1), o_att.reshape(m, -1)], w_out_b, l, mode="resid",
                     out_dtype=F32, tm=1024, tn=1024, resid=(x2, mod6, 2), rows_per_batch=s)

        h2 = _norm_mod(x2.reshape(b, s, d), norm_ffn_w[l], mod6, 4, 3).reshape(m, d)
        u = _matmul([h2], w_up_b, l, mode="relu2", out_dtype=BF16, tm=1024, tn=1024)
        x2 = _matmul([u], w_down_b, l, mode="resid", out_dtype=F32, tm=1024, tn=1024, tk=4096,
                     resid=(x2, mod6, 5), rows_per_batch=s)
    return x2.reshape(b, s, d)
```

```python
import functools

import jax
import jax.numpy as jnp
from jax import lax
from jax.experimental import pallas as pl
from jax.experimental.pallas import tpu as pltpu

HEAD_DIM = 128
CHUNK = 64
SUB = 16
N_SUB = CHUNK // SUB
N_MOD = 6
EPS = 1e-6
MASK_VALUE = -1e30
K_MAX = 1.0 - 1e-6
LOG2E = 1.4426950408889634
VMEM_LIMIT_BYTES = 60 * 1024 * 1024

F32 = jnp.float32
BF16 = jnp.bfloat16

_NT = (((1,), (1,)), ((), ()))
_TN = (((0,), (0,)), ((), ()))


def _params(*sem):
    return pltpu.CompilerParams(dimension_semantics=sem, vmem_limit_bytes=VMEM_LIMIT_BYTES)


def _fit_tile(tile, dim):
    tile = min(tile, dim)
    while dim % tile:
        tile -= HEAD_DIM
    return tile


def _sigmoid(x):
    return 1.0 / (1.0 + jnp.exp(-x))


def _tril_cumsum(tril, x):
    hi = x.astype(BF16)
    r1 = x - hi.astype(F32)
    mid = r1.astype(BF16)
    lo = (r1 - mid.astype(F32)).astype(BF16)
    out = jnp.dot(tril, hi, preferred_element_type=F32)
    out = out + jnp.dot(tril, mid, preferred_element_type=F32)
    return out + jnp.dot(tril, lo, preferred_element_type=F32)


def _tril(n):
    r = lax.broadcasted_iota(jnp.int32, (n, n), 0)
    c = lax.broadcasted_iota(jnp.int32, (n, n), 1)
    return (r >= c).astype(BF16)


def _mod_kernel(c_ref, w_ref, b_ref, o_ref):
    c = c_ref[...]
    ca = (c * _sigmoid(c)).astype(BF16)
    w = w_ref[...].astype(BF16)
    o_ref[...] = jnp.dot(ca, w, preferred_element_type=F32) + b_ref[...]


def _adaln_mod(c_pad, w_ada, b_ada):
    depth, d, n = w_ada.shape
    rows = c_pad.shape[0]
    tn = min(512, n)
    return pl.pallas_call(
        _mod_kernel,
        grid=(depth, n // tn),
        in_specs=[
            pl.BlockSpec((rows, d), lambda l, j: (0, 0)),
            pl.BlockSpec((None, d, tn), lambda l, j: (l, 0, j)),
            pl.BlockSpec((None, 1, tn), lambda l, j: (l, 0, j)),
        ],
        out_specs=pl.BlockSpec((None, rows, tn), lambda l, j: (l, 0, j)),
        out_shape=jax.ShapeDtypeStruct((depth, rows, n), F32),
        compiler_params=_params("arbitrary", "arbitrary"),
        name="adaln_mod",
    )(c_pad, w_ada, b_ada.reshape(depth, 1, n))


def _norm_mod_kernel(x_ref, w_ref, scale_ref, shift_ref, o_ref):
    x = x_ref[...]
    ms = jnp.mean(x * x, axis=-1, keepdims=True)
    y = x * lax.rsqrt(ms + EPS) * w_ref[...]
    o_ref[...] = (y * (1.0 + scale_ref[...]) + shift_ref[...]).astype(o_ref.dtype)


def _norm_mod(x3, w, mod6, scale_idx, shift_idx):
    b, s, d = x3.shape
    ts = min(512, s)
    return pl.pallas_call(
        _norm_mod_kernel,
        grid=(b, s // ts),
        in_specs=[
            pl.BlockSpec((None, ts, d), lambda i, t: (i, t, 0)),
            pl.BlockSpec((1, d), lambda i, t: (0, 0)),
            pl.BlockSpec((None, 1, d), lambda i, t: (i * N_MOD + scale_idx, 0, 0)),
            pl.BlockSpec((None, 1, d), lambda i, t: (i * N_MOD + shift_idx, 0, 0)),
        ],
        out_specs=pl.BlockSpec((None, ts, d), lambda i, t: (i, t, 0)),
        out_shape=jax.ShapeDtypeStruct((b, s, d), BF16),
        compiler_params=_params("arbitrary", "arbitrary"),
        name="norm_mod",
    )(x3, w.reshape(1, d), mod6, mod6)


def _slab_dots(a_refs, w_ref):
    part = None
    off = 0
    for a_ref in a_refs:
        ka = a_ref.shape[1]
        p = jnp.dot(a_ref[...], w_ref[off:off + ka, :], preferred_element_type=F32)
        part = p if part is None else part + p
        off += ka
    return part


def _mm_epilogue(mode, acc, o_ref, x_ref, g_ref):
    if mode == "plain":
        o_ref[...] = acc.astype(o_ref.dtype)
    elif mode == "relu2":
        r = jnp.maximum(acc, 0.0)
        o_ref[...] = (r * r).astype(o_ref.dtype)
    else:
        o_ref[...] = x_ref[...] + g_ref[...] * acc


W_CAST_ROWS = 512


def _mm_wcast_kernel(*refs, n_a, mode, layer, n_col_tiles):
    a_refs = refs[:n_a]
    w_hbm = refs[n_a]
    if mode == "resid":
        x_ref, g_ref, o_ref, wf_s, wb_s, sem = refs[n_a + 1:]
    else:
        o_ref, wf_s, wb_s, sem = refs[n_a + 1:]
        x_ref = g_ref = None
    j = pl.program_id(0)
    tn = wf_s.shape[1]

    def tile_copy(jj):
        cols = pl.ds(pl.multiple_of(jj * tn, tn), tn)
        return pltpu.make_async_copy(w_hbm.at[layer, :, cols], wf_s, sem)

    @pl.when(pl.program_id(1) == 0)
    def _():
        @pl.when(j == 0)
        def _():
            tile_copy(0).start()

        tile_copy(j).wait()
        for k0 in range(0, wf_s.shape[0], W_CAST_ROWS):
            wb_s[k0:k0 + W_CAST_ROWS, :] = wf_s[k0:k0 + W_CAST_ROWS, :].astype(BF16)

        @pl.when(j + 1 < n_col_tiles)
        def _():
            tile_copy(j + 1).start()

    _mm_epilogue(mode, _slab_dots(a_refs, wb_s), o_ref, x_ref, g_ref)


def _matmul_wcast(a_list, w_stack, layer, *, mode, out_dtype, tm, tn, n_cols=None,
                  resid=None, rows_per_batch=None):
    m = a_list[0].shape[0]
    _, kdim, n_total = w_stack.shape
    n = n_total if n_cols is None else n_cols
    n_a = len(a_list)
    tm = _fit_tile(tm, m)
    tn = _fit_tile(tn, n)
    assert m % tm == 0 and n % tn == 0 and kdim % W_CAST_ROWS == 0
    assert sum(a.shape[1] for a in a_list) == kdim

    in_specs = [pl.BlockSpec((tm, a.shape[1]), lambda j, i: (i, 0)) for a in a_list]
    in_specs.append(pl.BlockSpec(memory_space=pl.ANY))
    args = list(a_list) + [w_stack]
    if mode == "resid":
        x2, mod6, gate_idx = resid
        assert rows_per_batch % tm == 0
        tiles_per_batch = rows_per_batch // tm
        in_specs.append(pl.BlockSpec((tm, tn), lambda j, i: (i, j)))
        in_specs.append(pl.BlockSpec(
            (None, 1, tn), lambda j, i: ((i // tiles_per_batch) * N_MOD + gate_idx, 0, j)))
        args += [x2, mod6]
    return pl.pallas_call(
        functools.partial(_mm_wcast_kernel, n_a=n_a, mode=mode, layer=layer,
                          n_col_tiles=n // tn),
        grid=(n // tn, m // tm),
        in_specs=in_specs,
        out_specs=pl.BlockSpec((tm, tn), lambda j, i: (i, j)),
        out_shape=jax.ShapeDtypeStruct((m, n), out_dtype),
        scratch_shapes=[pltpu.VMEM((kdim, tn), F32), pltpu.VMEM((kdim, tn), BF16),
                        pltpu.SemaphoreType.DMA(())],
        compiler_params=_params("arbitrary", "arbitrary"),
        name="mmw_" + mode,
    )(*args)


def _mm_kernel(*refs, n_a, nk, mode):
    a_refs = refs[:n_a]
    w_ref = refs[n_a]
    if mode == "resid":
        x_ref, g_ref, o_ref = refs[n_a + 1:]
    else:
        o_ref, = refs[n_a + 1:]
        x_ref = g_ref = None

    part = _slab_dots(a_refs, w_ref)

    def finish(acc):
        _mm_epilogue(mode, acc, o_ref, x_ref, g_ref)

    if nk == 1:
        finish(part)
    else:
        k = pl.program_id(2)

        @pl.when(k == 0)
        def _():
            o_ref[...] = part

        @pl.when(jnp.logical_and(k > 0, k < nk - 1))
        def _():
            o_ref[...] += part

        @pl.when(k == nk - 1)
        def _():
            finish(o_ref[...] + part)


def _matmul(a_list, w_stack, layer, *, mode, out_dtype, tm, tn, tk=None, n_cols=None,
            resid=None, rows_per_batch=None):
    m = a_list[0].shape[0]
    _, kdim, n_total = w_stack.shape
    n = n_total if n_cols is None else n_cols
    n_a = len(a_list)
    tm = _fit_tile(tm, m)
    tn = _fit_tile(tn, n)
    if tk is None or tk >= kdim:
        tk = kdim
    nk = kdim // tk
    assert n_a == 1 or nk == 1
    assert nk == 1 or out_dtype == F32
    assert m % tm == 0 and n % tn == 0 and kdim % tk == 0

    in_specs = []
    if n_a == 1:
        in_specs.append(pl.BlockSpec((tm, tk), lambda i, j, k: (i, k)))
    else:
        for a in a_list:
            in_specs.append(pl.BlockSpec((tm, a.shape[1]), lambda i, j, k: (i, 0)))
    in_specs.append(pl.BlockSpec((None, tk, tn), lambda i, j, k: (layer, k, j)))
    args = list(a_list) + [w_stack]
    if mode == "resid":
        x2, mod6, gate_idx = resid
        assert rows_per_batch % tm == 0
        tiles_per_batch = rows_per_batch // tm
        in_specs.append(pl.BlockSpec((tm, tn), lambda i, j, k: (i, j)))
        in_specs.append(pl.BlockSpec(
            (None, 1, tn), lambda i, j, k: ((i // tiles_per_batch) * N_MOD + gate_idx, 0, j)))
        args += [x2, mod6]
    return pl.pallas_call(
        functools.partial(_mm_kernel, n_a=n_a, nk=nk, mode=mode),
        grid=(m // tm, n // tn, nk),
        in_specs=in_specs,
        out_specs=pl.BlockSpec((tm, tn), lambda i, j, k: (i, j)),
        out_shape=jax.ShapeDtypeStruct((m, n), out_dtype),
        compiler_params=_params("arbitrary", "arbitrary", "arbitrary"),
        name="mm_" + mode,
    )(*args)


def _fcum_kernel(fg_ref, bias_ref, fc_ref):
    n_slab = fg_ref.shape[0] // HEAD_DIM
    tril = _tril(HEAD_DIM)

    def body(r, carry):
        rows = pl.ds(pl.multiple_of(r * HEAD_DIM, HEAD_DIM), HEAD_DIM)
        z = fg_ref[rows, :] + bias_ref[...]
        lf = jnp.minimum(z, 0.0) - jnp.log1p(jnp.exp(-jnp.abs(z)))
        f = _tril_cumsum(tril, lf) + carry
        fc_ref[rows, :] = f
        return f[HEAD_DIM - 1:HEAD_DIM, :]

    lax.fori_loop(0, n_slab, body, jnp.zeros((1, HEAD_DIM), F32))


def _fcum(fg3, bias_row):
    b, s, _ = fg3.shape
    return pl.pallas_call(
        _fcum_kernel,
        grid=(b,),
        in_specs=[
            pl.BlockSpec((None, s, HEAD_DIM), lambda i: (i, 0, 0)),
            pl.BlockSpec((1, HEAD_DIM), lambda i: (0, 0)),
        ],
        out_specs=pl.BlockSpec((None, s, HEAD_DIM), lambda i: (i, 0, 0)),
        out_shape=jax.ShapeDtypeStruct((b, s, HEAD_DIM), F32),
        compiler_params=_params("arbitrary"),
        name="fox_fcum",
    )(fg3, bias_row)


V_ROWS = HEAD_DIM + 16


def _fox_kernel(q_ref, k_ref, v_ref, fc_ref, qn_ref, kn_ref, o_ref,
                kaug_s, vaug_s, qaug_s, s0_s, s1_s, m_s, acc_s, *, blk, n_blk):
    h = pl.program_id(1)
    qi = pl.program_id(2)
    lane = lax.broadcasted_iota(jnp.int32, (blk, HEAD_DIM), 1)

    @pl.when(qi == 0)
    def _():
        ones_row = (lax.broadcasted_iota(jnp.int32, (V_ROWS - HEAD_DIM, blk), 0) == 0)
        for c in range(n_blk):
            rows = slice(c * blk, (c + 1) * blk)
            k = k_ref[rows, :]
            ms = jnp.mean(k * k, axis=-1, keepdims=True)
            kaug_s[rows, 0:HEAD_DIM] = (k * lax.rsqrt(ms + EPS) * kn_ref[...]).astype(BF16)
            fcol = jnp.sum(jnp.where(lane == h, fc_ref[rows, :], 0.0), axis=1, keepdims=True)
            nf = fcol * (-LOG2E)
            hi = nf.astype(BF16).astype(F32)
            mid = (nf - hi).astype(BF16).astype(F32)
            lo = (nf - hi) - mid
            aug = jnp.where(lane == 0, hi, jnp.where(lane == 1, mid, jnp.where(lane == 2, lo, 0.0)))
            kaug_s[rows, HEAD_DIM:2 * HEAD_DIM] = aug.astype(BF16)
            vaug_s[c, 0:HEAD_DIM, :] = v_ref[rows, :].T.astype(BF16)
            vaug_s[c, HEAD_DIM:V_ROWS, :] = ones_row.astype(BF16)

    q = q_ref[...]
    ms = jnp.mean(q * q, axis=-1, keepdims=True)
    qaug_s[:, 0:HEAD_DIM] = (q * lax.rsqrt(ms + EPS) * qn_ref[...]
                             * (HEAD_DIM ** -0.5 * LOG2E)).astype(BF16)
    qaug_s[:, HEAD_DIM:2 * HEAD_DIM] = (lane < 3).astype(BF16)
    m_s[...] = jnp.full(m_s.shape, MASK_VALUE, F32)
    acc_s[...] = jnp.zeros(acc_s.shape, F32)

    def scores(kj):
        rows = pl.ds(pl.multiple_of(kj * blk, blk), blk)
        return lax.dot_general(kaug_s[rows, :], qaug_s[...], _NT, preferred_element_type=F32)

    def update(s_ref, kj, causal):
        st = s_ref[...]
        if causal:
            kr = lax.broadcasted_iota(jnp.int32, (blk, blk), 0)
            qc = lax.broadcasted_iota(jnp.int32, (blk, blk), 1)
            st = jnp.where(kr <= qc, st, MASK_VALUE)
        m_old = m_s[...]
        m_new = jnp.maximum(m_old, jnp.max(st, axis=0, keepdims=True))
        pt = jnp.exp2(st - m_new).astype(BF16)
        alpha = jnp.exp2(m_old - m_new)
        acc_s[...] = alpha * acc_s[...] + jnp.dot(vaug_s[kj], pt, preferred_element_type=F32)
        m_s[...] = m_new

    s0_s[...] = scores(0)

    def pair(i, carry):
        kj = 2 * i
        s1_s[...] = scores(kj + 1)
        update(s0_s, kj, False)
        s0_s[...] = scores(kj + 2)
        update(s1_s, kj + 1, False)
        return carry

    lax.fori_loop(0, qi // 2, pair, 0)

    @pl.when(qi % 2 == 0)
    def _():
        update(s0_s, qi, True)

    @pl.when(qi % 2 == 1)
    def _():
        s1_s[...] = scores(qi)
        update(s0_s, qi - 1, False)
        update(s1_s, qi, True)

    acc = acc_s[...]
    ot = acc[0:HEAD_DIM, :] / acc[HEAD_DIM:HEAD_DIM + 1, :]
    o_ref[...] = ot.T.astype(o_ref.dtype)


def _fox(proj3, fc, qn_w, kn_w, *, n_heads, q_blk0, k_blk0, v_blk0, blk):
    b, s, _ = proj3.shape
    n_blk = s // blk
    return pl.pallas_call(
        functools.partial(_fox_kernel, blk=blk, n_blk=n_blk),
        grid=(b, n_heads, n_blk),
        in_specs=[
            pl.BlockSpec((None, blk, HEAD_DIM), lambda i, h, t: (i, t, q_blk0 + h)),
            pl.BlockSpec((None, s, HEAD_DIM), lambda i, h, t: (i, 0, k_blk0 + h)),
            pl.BlockSpec((None, s, HEAD_DIM), lambda i, h, t: (i, 0, v_blk0 + h)),
            pl.BlockSpec((None, s, HEAD_DIM), lambda i, h, t: (i, 0, 0)),
            pl.BlockSpec((1, HEAD_DIM), lambda i, h, t: (0, 0)),
            pl.BlockSpec((1, HEAD_DIM), lambda i, h, t: (0, 0)),
        ],
        out_specs=pl.BlockSpec((None, blk, HEAD_DIM), lambda i, h, t: (i, t, h)),
        out_shape=jax.ShapeDtypeStruct((b, s, n_heads * HEAD_DIM), BF16),
        scratch_shapes=[
            pltpu.VMEM((s, 2 * HEAD_DIM), BF16),
            pltpu.VMEM((n_blk, V_ROWS, blk), BF16),
            pltpu.VMEM((blk, 2 * HEAD_DIM), BF16),
            pltpu.VMEM((blk, blk), F32),
            pltpu.VMEM((blk, blk), F32),
            pltpu.VMEM((1, blk), F32),
            pltpu.VMEM((V_ROWS, blk), F32),
        ],
        compiler_params=_params("arbitrary", "arbitrary", "arbitrary"),
        name="fox_attn",
    )(proj3, proj3, proj3, fc, qn_w.reshape(1, HEAD_DIM), kn_w.reshape(1, HEAD_DIM))


def _hgrn2_kernel(q_ref, f_ref, i_ref, g_ref, lb_ref, nw_ref, seg_ref, o_ref,
                  st_s, y_s, g_s, kk_s, qf_s, r_s, a_s, qe_s, u_s, dec_s, *, layer, rows):
    t_id = pl.program_id(2)
    n_chunk = rows // CHUNK

    @pl.when(t_id == 0)
    def _():
        st_s[...] = jnp.zeros(st_s.shape, F32)

    depth = lb_ref.shape[0]
    lbs = [lb_ref[j] for j in range(depth)]
    mx = lbs[0]
    for j in range(1, depth):
        mx = jnp.maximum(mx, lbs[j])
    es = [jnp.exp(v - mx) for v in lbs]
    den = es[0]
    for j in range(1, depth):
        den = den + es[j]
    lb = jnp.zeros((1, HEAD_DIM), F32)
    for j in range(1, layer + 1):
        lb = lb + es[j] / den

    tril = _tril(CHUNK)

    half = SUB // 2
    zeros_half = jnp.zeros((half, HEAD_DIM), F32)
    row_half = lax.broadcasted_iota(jnp.int32, (half, HEAD_DIM), 0)

    def prep(c, carry):
        rws = pl.ds(pl.multiple_of(c * CHUNK, CHUNK), CHUNK)
        z = f_ref[rws, :]
        kk = (1.0 - lb) * _sigmoid(-z)
        logf = jnp.log1p(-jnp.minimum(kk, K_MAX))
        gc = _tril_cumsum(tril, logf) * LOG2E
        q = q_ref[rws, :]
        qf = q * _sigmoid(q)
        kk_s[rws, :] = kk
        g_s[rws, :] = gc
        qf_s[rws, :] = qf
        for b in range(N_SUB):
            r0 = b * SUB
            q_t, q_b = qf[r0:r0 + half], qf[r0 + half:r0 + SUB]
            k_t, k_b = kk[r0:r0 + half], kk[r0 + half:r0 + SUB]
            g_t, g_b = gc[r0:r0 + half], gc[r0 + half:r0 + SUB]
            m_t = g_t[half // 2:half // 2 + 1]
            m_b = g_b[half // 2:half // 2 + 1]
            g_8 = g_b[0:1]
            qt_f = q_t * jnp.exp2(g_t - m_t)
            pt_f = k_t * jnp.exp2(m_t - g_t)
            qb_f = q_b * jnp.exp2(g_b - m_b)
            pb_f = k_b * jnp.exp2(m_b - g_b)
            qx_f = q_b * jnp.exp2(g_b - g_8)
            px_f = k_t * jnp.exp2(g_8 - g_t)
            yrows = pl.ds(pl.multiple_of(c * CHUNK + r0, SUB), SUB)
            for s in range(SUB):
                if s < half:
                    top = jnp.where(row_half >= s, qt_f * pt_f[s:s + 1], 0.0)
                    bot = qx_f * px_f[s:s + 1]
                else:
                    top = zeros_half
                    bot = jnp.where(row_half >= s - half, qb_f * pb_f[s - half:s - half + 1], 0.0)
                x = jnp.concatenate([top, bot], axis=0)
                y_s[yrows, s * HEAD_DIM:(s + 1) * HEAD_DIM] = x.astype(BF16)
        return carry

    lax.fori_loop(0, n_chunk, prep, 0, unroll=True)

    r_s[...] = jnp.dot(y_s[...], seg_ref[...], preferred_element_type=F32)

    tr = lax.broadcasted_iota(jnp.int32, (CHUNK, CHUNK), 0)
    tc = lax.broadcasted_iota(jnp.int32, (CHUNK, CHUNK), 1)
    same_blk = (tr // SUB) == (tc // SUB)
    diag_mask = jnp.logical_and(same_blk, tc <= tr)
    off_mask = (tc // SUB) < (tr // SUB)

    def scores(c, carry):
        rws = pl.ds(pl.multiple_of(c * CHUNK, CHUNK), CHUNK)
        qf = qf_s[rws, :]
        kk = kk_s[rws, :]
        gc = g_s[rws, :]
        blocks = [jnp.zeros((SUB, CHUNK), F32)]
        for i in range(1, N_SUB):
            gn = gc[i * SUB:i * SUB + 1]
            ql = qf[i * SUB:(i + 1) * SUB] * jnp.exp2(gc[i * SUB:(i + 1) * SUB] - gn)
            kr = kk * jnp.exp2(jnp.minimum(gn - gc, 0.0))
            blocks.append(lax.dot_general(ql.astype(BF16), kr.astype(BF16), _NT,
                                          preferred_element_type=F32))
        a_off = jnp.concatenate(blocks, axis=0)
        a = jnp.where(diag_mask, r_s[rws, :][:, :CHUNK], jnp.where(off_mask, a_off, 0.0))
        a_s[rws, :] = a.astype(BF16)
        g_last = gc[CHUNK - 1:CHUNK]
        qe_s[rws, :] = (qf * jnp.exp2(gc)).astype(BF16)
        kd = kk * jnp.exp2(g_last - gc)
        u_s[c] = lax.dot_general(i_ref[rws, :].astype(BF16), kd.astype(BF16), _TN,
                                 preferred_element_type=F32)
        dec_s[c] = jnp.exp2(g_last)
        return carry

    lax.fori_loop(0, n_chunk, scores, 0, unroll=True)

    def chunk(c, carry):
        rws = pl.ds(pl.multiple_of(c * CHUNK, CHUNK), CHUNK)
        vb = i_ref[rws, :].astype(BF16)
        st = st_s[...]
        o = jnp.dot(a_s[rws, :], vb, preferred_element_type=F32)
        o = o + lax.dot_general(qe_s[rws, :], st.astype(BF16), _NT,
                                preferred_element_type=F32)
        st_s[...] = dec_s[c] * st + u_s[c]
        ms = jnp.mean(o * o, axis=-1, keepdims=True)
        y = o * lax.rsqrt(ms + EPS) * nw_ref[...]
        g = g_ref[rws, :]
        o_ref[rws, :] = (y * (g * _sigmoid(g))).astype(o_ref.dtype)
        return carry

    lax.fori_loop(0, n_chunk, chunk, 0, unroll=True)


def _hgrn2(proj3, lower_bounds, nw, *, layer, n_heads, rows):
    b, s, _ = proj3.shape
    depth = lower_bounds.shape[0]
    rows = min(rows, s)
    seg_r = jnp.arange(SUB * HEAD_DIM, dtype=jnp.int32)[:, None] // HEAD_DIM
    seg_c = jnp.arange(HEAD_DIM, dtype=jnp.int32)[None, :]
    seg = jnp.logical_and(seg_c < CHUNK, (seg_c % SUB) == seg_r).astype(BF16)
    lb4 = lower_bounds.astype(F32).reshape(depth, n_heads, 1, HEAD_DIM)

    def col(off):
        return pl.BlockSpec((None, rows, HEAD_DIM), lambda i, h, t: (i, t, off * n_heads + h))

    return pl.pallas_call(
        functools.partial(_hgrn2_kernel, layer=layer, rows=rows),
        grid=(b, n_heads, s // rows),
        in_specs=[
            col(0), col(1), col(2), col(3),
            pl.BlockSpec((depth, None, 1, HEAD_DIM), lambda i, h, t: (0, h, 0, 0)),
            pl.BlockSpec((1, HEAD_DIM), lambda i, h, t: (0, 0)),
            pl.BlockSpec((SUB * HEAD_DIM, HEAD_DIM), lambda i, h, t: (0, 0)),
        ],
        out_specs=pl.BlockSpec((None, rows, HEAD_DIM), lambda i, h, t: (i, t, h)),
        out_shape=jax.ShapeDtypeStruct((b, s, n_heads * HEAD_DIM), BF16),
        scratch_shapes=[
            pltpu.VMEM((HEAD_DIM, HEAD_DIM), F32),
            pltpu.VMEM((rows, SUB * HEAD_DIM), BF16),
            pltpu.VMEM((rows, HEAD_DIM), F32),
            pltpu.VMEM((rows, HEAD_DIM), F32),
            pltpu.VMEM((rows, HEAD_DIM), F32),
            pltpu.VMEM((rows, HEAD_DIM), F32),
            pltpu.VMEM((rows, CHUNK), BF16),
            pltpu.VMEM((rows, HEAD_DIM), BF16),
            pltpu.VMEM((rows // CHUNK, HEAD_DIM, HEAD_DIM), F32),
            pltpu.VMEM((rows // CHUNK, 1, HEAD_DIM), F32),
        ],
        compiler_params=_params("arbitrary", "arbitrary", "arbitrary"),
        name="hgrn2",
    )(proj3, proj3, proj3, proj3, lb4, nw.reshape(1, HEAD_DIM), seg)


def kernel(x, c, lower_bounds, w_ada, b_ada, norm_mix_w, norm_ffn_w, w_in, rec_norm_w,
           fg_bias, q_norm_w, k_norm_w, w_out, w_up, w_down):
    b, s, d = x.shape
    depth = w_ada.shape[0]
    n_heads = d // HEAD_DIM
    h_rec = n_heads // 2
    h_att = n_heads - h_rec
    main_cols = (4 * h_rec + 3 * h_att) * HEAD_DIM
    assert w_in.shape[2] == main_cols + h_att and h_att <= HEAD_DIM
    assert s % CHUNK == 0 and s % HEAD_DIM == 0
    m = b * s
    fox_blk = min(512, s)

    rows_pad = -(-b // 8) * 8
    c_pad = jnp.zeros((rows_pad, d), F32).at[:b].set(c.astype(F32))
    mod = _adaln_mod(c_pad, w_ada, b_ada)

    w_fg_b = jnp.pad(w_in[:, :, main_cols:],
                     ((0, 0), (0, 0), (0, HEAD_DIM - h_att))).astype(BF16)
    w_down_b = w_down.astype(BF16)

    x2 = x.reshape(m, d)
    for l in range(depth):
        mod6 = mod[l, :b].reshape(b * N_MOD, 1, d)

        h = _norm_mod(x2.reshape(b, s, d), norm_mix_w[l], mod6, 1, 0).reshape(m, d)
        proj = _matmul_wcast([h], w_in, l, mode="plain", out_dtype=F32, tm=1024, tn=1024,
                             n_cols=main_cols)
        fg = _matmul([h], w_fg_b, l, mode="plain", out_dtype=F32, tm=1024, tn=HEAD_DIM)
        proj3 = proj.reshape(b, s, main_cols)

        o_rec = _hgrn2(proj3, lower_bounds, rec_norm_w[l], layer=l, n_heads=h_rec, rows=1024)

        bias_row = jnp.pad(fg_bias[l].astype(F32), (0, HEAD_DIM - h_att)).reshape(1, HEAD_DIM)
        fc = _fcum(fg.reshape(b, s, HEAD_DIM), bias_row)
        o_att = _fox(proj3, fc, q_norm_w[l], k_norm_w[l], n_heads=h_att,
                     q_blk0=4 * h_rec, k_blk0=4 * h_rec + h_att, v_blk0=4 * h_rec + 2 * h_att,
                     blk=fox_blk)

        x2 = _matmul_wcast([o_rec.reshape(m, -1), o_att.reshape(m, -1)], w_out, l, mode="resid",
                           out_dtype=F32, tm=1024, tn=512, resid=(x2, mod6, 2),
                           rows_per_batch=s)

        h2 = _norm_mod(x2.reshape(b, s, d), norm_ffn_w[l], mod6, 4, 3).reshape(m, d)
        u = _matmul_wcast([h2], w_up, l, mode="relu2", out_dtype=BF16, tm=1024, tn=1024)
        x2 = _matmul([u], w_down_b, l, mode="resid", out_dtype=F32, tm=1024, tn=1024, tk=4096,
                     resid=(x2, mod6, 5), rows_per_batch=s)
    return x2.reshape(b, s, d)
```

```python
import functools

import jax
import jax.numpy as jnp
from jax import lax
from jax.experimental import pallas as pl
from jax.experimental.pallas import tpu as pltpu

HEAD_DIM = 128
CHUNK = 64
SUB = 16
N_SUB = CHUNK // SUB
N_MOD = 6
EPS = 1e-6
MASK_VALUE = -1e30
K_MAX = 1.0 - 1e-6
LOG2E = 1.4426950408889634
VMEM_LIMIT_BYTES = 60 * 1024 * 1024

F32 = jnp.float32
BF16 = jnp.bfloat16

_NT = (((1,), (1,)), ((), ()))
_TN = (((0,), (0,)), ((), ()))


def _params(*sem):
    return pltpu.CompilerParams(dimension_semantics=sem, vmem_limit_bytes=VMEM_LIMIT_BYTES)


def _fit_tile(tile, dim):
    tile = min(tile, dim)
    while dim % tile:
        tile -= HEAD_DIM
    return tile


def _sigmoid(x):
    return 1.0 / (1.0 + jnp.exp(-x))


def _tril_cumsum(tril, x):
    hi = x.astype(BF16)
    r1 = x - hi.astype(F32)
    mid = r1.astype(BF16)
    lo = (r1 - mid.astype(F32)).astype(BF16)
    out = jnp.dot(tril, hi, preferred_element_type=F32)
    out = out + jnp.dot(tril, mid, preferred_element_type=F32)
    return out + jnp.dot(tril, lo, preferred_element_type=F32)


def _tril(n):
    r = lax.broadcasted_iota(jnp.int32, (n, n), 0)
    c = lax.broadcasted_iota(jnp.int32, (n, n), 1)
    return (r >= c).astype(BF16)


def _mod_kernel(c_ref, w_ref, b_ref, o_ref):
    c = c_ref[...]
    ca = (c * _sigmoid(c)).astype(BF16)
    w = w_ref[...].astype(BF16)
    o_ref[...] = jnp.dot(ca, w, preferred_element_type=F32) + b_ref[...]


def _adaln_mod(c_pad, w_ada, b_ada):
    depth, d, n = w_ada.shape
    rows = c_pad.shape[0]
    tn = min(512, n)
    return pl.pallas_call(
        _mod_kernel,
        grid=(depth, n // tn),
        in_specs=[
            pl.BlockSpec((rows, d), lambda l, j: (0, 0)),
            pl.BlockSpec((None, d, tn), lambda l, j: (l, 0, j)),
            pl.BlockSpec((None, 1, tn), lambda l, j: (l, 0, j)),
        ],
        out_specs=pl.BlockSpec((None, rows, tn), lambda l, j: (l, 0, j)),
        out_shape=jax.ShapeDtypeStruct((depth, rows, n), F32),
        compiler_params=_params("arbitrary", "arbitrary"),
        name="adaln_mod",
    )(c_pad, w_ada, b_ada.reshape(depth, 1, n))


def _norm_mod_kernel(x_ref, w_ref, scale_ref, shift_ref, o_ref):
    x = x_ref[...]
    ms = jnp.mean(x * x, axis=-1, keepdims=True)
    y = x * lax.rsqrt(ms + EPS) * w_ref[...]
    o_ref[...] = (y * (1.0 + scale_ref[...]) + shift_ref[...]).astype(o_ref.dtype)


def _norm_mod(x3, w, mod6, scale_idx, shift_idx):
    b, s, d = x3.shape
    ts = min(512, s)
    return pl.pallas_call(
        _norm_mod_kernel,
        grid=(b, s // ts),
        in_specs=[
            pl.BlockSpec((None, ts, d), lambda i, t: (i, t, 0)),
            pl.BlockSpec((1, d), lambda i, t: (0, 0)),
            pl.BlockSpec((None, 1, d), lambda i, t: (i * N_MOD + scale_idx, 0, 0)),
            pl.BlockSpec((None, 1, d), lambda i, t: (i * N_MOD + shift_idx, 0, 0)),
        ],
        out_specs=pl.BlockSpec((None, ts, d), lambda i, t: (i, t, 0)),
        out_shape=jax.ShapeDtypeStruct((b, s, d), BF16),
        compiler_params=_params("arbitrary", "arbitrary"),
        name="norm_mod",
    )(x3, w.reshape(1, d), mod6, mod6)


def _slab_dots(a_refs, w_ref, w_t=False):
    part = None
    off = 0
    for a_ref in a_refs:
        ka = a_ref.shape[1]
        if w_t:
            p = lax.dot_general(a_ref[...], w_ref[:, off:off + ka], _NT,
                                preferred_element_type=F32)
        else:
            p = jnp.dot(a_ref[...], w_ref[off:off + ka, :], preferred_element_type=F32)
        part = p if part is None else part + p
        off += ka
    return part


def _mm_epilogue(mode, acc, o_ref, x_ref, g_ref):
    if mode == "plain":
        o_ref[...] = acc.astype(o_ref.dtype)
    elif mode == "relu2":
        r = jnp.maximum(acc, 0.0)
        o_ref[...] = (r * r).astype(o_ref.dtype)
    else:
        o_ref[...] = x_ref[...] + g_ref[...] * acc


W_CAST_ROWS = 512


def _mm_wcast_kernel(*refs, n_a, mode, layer, n_col_tiles, tn, w_t):
    a_refs = refs[:n_a]
    w_hbm = refs[n_a]
    if mode == "resid":
        x_ref, g_ref, o_ref, wf_s, wb_s, sem = refs[n_a + 1:]
    else:
        o_ref, wf_s, wb_s, sem = refs[n_a + 1:]
        x_ref = g_ref = None
    j = pl.program_id(0)

    def tile_copy(jj):
        sel = pl.ds(pl.multiple_of(jj * tn, tn), tn)
        src = w_hbm.at[layer, sel, :] if w_t else w_hbm.at[layer, :, sel]
        return pltpu.make_async_copy(src, wf_s, sem)

    @pl.when(pl.program_id(1) == 0)
    def _():
        @pl.when(j == 0)
        def _():
            tile_copy(0).start()

        tile_copy(j).wait()
        n_rows = wf_s.shape[0]
        for r0 in range(0, n_rows, W_CAST_ROWS):
            r1 = min(r0 + W_CAST_ROWS, n_rows)
            wb_s[r0:r1, :] = wf_s[r0:r1, :].astype(BF16)

        @pl.when(j + 1 < n_col_tiles)
        def _():
            tile_copy(j + 1).start()

    _mm_epilogue(mode, _slab_dots(a_refs, wb_s, w_t), o_ref, x_ref, g_ref)


def _matmul_wcast(a_list, w_stack, layer, *, mode, out_dtype, tm, tn, n_cols=None, w_t=False,
                  resid=None, rows_per_batch=None):
    m = a_list[0].shape[0]
    if w_t:
        _, n_total, kdim = w_stack.shape
    else:
        _, kdim, n_total = w_stack.shape
    n = n_total if n_cols is None else n_cols
    n_a = len(a_list)
    tm = _fit_tile(tm, m)
    tn = _fit_tile(tn, n)
    w_tile = (tn, kdim) if w_t else (kdim, tn)
    assert m % tm == 0 and n % tn == 0
    assert sum(a.shape[1] for a in a_list) == kdim

    in_specs = [pl.BlockSpec((tm, a.shape[1]), lambda j, i: (i, 0)) for a in a_list]
    in_specs.append(pl.BlockSpec(memory_space=pl.ANY))
    args = list(a_list) + [w_stack]
    if mode == "resid":
        x2, mod6, gate_idx = resid
        assert rows_per_batch % tm == 0
        tiles_per_batch = rows_per_batch // tm
        in_specs.append(pl.BlockSpec((tm, tn), lambda j, i: (i, j)))
        in_specs.append(pl.BlockSpec(
            (None, 1, tn), lambda j, i: ((i // tiles_per_batch) * N_MOD + gate_idx, 0, j)))
        args += [x2, mod6]
    return pl.pallas_call(
        functools.partial(_mm_wcast_kernel, n_a=n_a, mode=mode, layer=layer,
                          n_col_tiles=n // tn, tn=tn, w_t=w_t),
        grid=(n // tn, m // tm),
        in_specs=in_specs,
        out_specs=pl.BlockSpec((tm, tn), lambda j, i: (i, j)),
        out_shape=jax.ShapeDtypeStruct((m, n), out_dtype),
        scratch_shapes=[pltpu.VMEM(w_tile, F32), pltpu.VMEM(w_tile, BF16),
                        pltpu.SemaphoreType.DMA(())],
        compiler_params=_params("arbitrary", "arbitrary"),
        name="mmw_" + mode,
    )(*args)


def _fg_kernel(h_ref, w_ref, o_ref, *, n_valid):
    row = lax.broadcasted_iota(jnp.int32, w_ref.shape, 0)
    w = jnp.where(row < n_valid, w_ref[...], 0.0).astype(BF16)
    o_ref[...] = lax.dot_general(h_ref[...], w, _NT, preferred_element_type=F32)


def _fg_proj(h, w_in_t, layer, *, row0, tm):
    m, kdim = h.shape
    n_valid = w_in_t.shape[1] - row0
    assert row0 % HEAD_DIM == 0 and 0 < n_valid <= HEAD_DIM
    tm = _fit_tile(tm, m)
    return pl.pallas_call(
        functools.partial(_fg_kernel, n_valid=n_valid),
        grid=(m // tm,),
        in_specs=[
            pl.BlockSpec((tm, kdim), lambda i: (i, 0)),
            pl.BlockSpec((None, HEAD_DIM, kdim), lambda i: (layer, row0 // HEAD_DIM, 0)),
        ],
        out_specs=pl.BlockSpec((tm, HEAD_DIM), lambda i: (i, 0)),
        out_shape=jax.ShapeDtypeStruct((m, HEAD_DIM), F32),
        compiler_params=_params("arbitrary"),
        name="fg_proj",
    )(h, w_in_t)


def _mm_kernel(*refs, n_a, nk, mode):
    a_refs = refs[:n_a]
    w_ref = refs[n_a]
    if mode == "resid":
        x_ref, g_ref, o_ref = refs[n_a + 1:]
    else:
        o_ref, = refs[n_a + 1:]
        x_ref = g_ref = None

    part = _slab_dots(a_refs, w_ref)

    def finish(acc):
        _mm_epilogue(mode, acc, o_ref, x_ref, g_ref)

    if nk == 1:
        finish(part)
    else:
        k = pl.program_id(2)

        @pl.when(k == 0)
        def _():
            o_ref[...] = part

        @pl.when(jnp.logical_and(k > 0, k < nk - 1))
        def _():
            o_ref[...] += part

        @pl.when(k == nk - 1)
        def _():
            finish(o_ref[...] + part)


def _matmul(a_list, w_stack, layer, *, mode, out_dtype, tm, tn, tk=None, n_cols=None,
            resid=None, rows_per_batch=None):
    m = a_list[0].shape[0]
    _, kdim, n_total = w_stack.shape
    n = n_total if n_cols is None else n_cols
    n_a = len(a_list)
    tm = _fit_tile(tm, m)
    tn = _fit_tile(tn, n)
    if tk is None or tk >= kdim:
        tk = kdim
    nk = kdim // tk
    assert n_a == 1 or nk == 1
    assert nk == 1 or out_dtype == F32
    assert m % tm == 0 and n % tn == 0 and kdim % tk == 0

    in_specs = []
    if n_a == 1:
        in_specs.append(pl.BlockSpec((tm, tk), lambda i, j, k: (i, k)))
    else:
        for a in a_list:
            in_specs.append(pl.BlockSpec((tm, a.shape[1]), lambda i, j, k: (i, 0)))
    in_specs.append(pl.BlockSpec((None, tk, tn), lambda i, j, k: (layer, k, j)))
    args = list(a_list) + [w_stack]
    if mode == "resid":
        x2, mod6, gate_idx = resid
        assert rows_per_batch % tm == 0
        tiles_per_batch = rows_per_batch // tm
        in_specs.append(pl.BlockSpec((tm, tn), lambda i, j, k: (i, j)))
        in_specs.append(pl.BlockSpec(
            (None, 1, tn), lambda i, j, k: ((i // tiles_per_batch) * N_MOD + gate_idx, 0, j)))
        args += [x2, mod6]
    return pl.pallas_call(
        functools.partial(_mm_kernel, n_a=n_a, nk=nk, mode=mode),
        grid=(m // tm, n // tn, nk),
        in_specs=in_specs,
        out_specs=pl.BlockSpec((tm, tn), lambda i, j, k: (i, j)),
        out_shape=jax.ShapeDtypeStruct((m, n), out_dtype),
        compiler_params=_params("arbitrary", "arbitrary", "arbitrary"),
        name="mm_" + mode,
    )(*args)


def _fcum_kernel(fg_ref, bias_ref, fc_ref):
    n_slab = fg_ref.shape[0] // HEAD_DIM
    tril = _tril(HEAD_DIM)

    def body(r, carry):
        rows = pl.ds(pl.multiple_of(r * HEAD_DIM, HEAD_DIM), HEAD_DIM)
        z = fg_ref[rows, :] + bias_ref[...]
        lf = jnp.minimum(z, 0.0) - jnp.log1p(jnp.exp(-jnp.abs(z)))
        f = _tril_cumsum(tril, lf) + carry
        fc_ref[rows, :] = f
        return f[HEAD_DIM - 1:HEAD_DIM, :]

    lax.fori_loop(0, n_slab, body, jnp.zeros((1, HEAD_DIM), F32))


def _fcum(fg3, bias_row):
    b, s, _ = fg3.shape
    return pl.pallas_call(
        _fcum_kernel,
        grid=(b,),
        in_specs=[
            pl.BlockSpec((None, s, HEAD_DIM), lambda i: (i, 0, 0)),
            pl.BlockSpec((1, HEAD_DIM), lambda i: (0, 0)),
        ],
        out_specs=pl.BlockSpec((None, s, HEAD_DIM), lambda i: (i, 0, 0)),
        out_shape=jax.ShapeDtypeStruct((b, s, HEAD_DIM), F32),
        compiler_params=_params("arbitrary"),
        name="fox_fcum",
    )(fg3, bias_row)


V_ROWS = HEAD_DIM + 16


def _fox_kernel(q_ref, k_ref, v_ref, fc_ref, qn_ref, kn_ref, o_ref,
                kaug_s, vaug_s, qaug_s, s0_s, s1_s, m_s, acc_s, *, blk, n_blk):
    h = pl.program_id(1)
    qi = pl.program_id(2)
    qblk = 2 * blk
    lane = lax.broadcasted_iota(jnp.int32, (blk, HEAD_DIM), 1)

    @pl.when(qi == 0)
    def _():
        ones_row = (lax.broadcasted_iota(jnp.int32, (V_ROWS - HEAD_DIM, blk), 0) == 0)
        for c in range(n_blk):
            rows = slice(c * blk, (c + 1) * blk)
            k = k_ref[rows, :]
            ms = jnp.mean(k * k, axis=-1, keepdims=True)
            kaug_s[rows, 0:HEAD_DIM] = (k * lax.rsqrt(ms + EPS) * kn_ref[...]).astype(BF16)
            fcol = jnp.sum(jnp.where(lane == h, fc_ref[rows, :], 0.0), axis=1, keepdims=True)
            nf = fcol * (-LOG2E)
            hi = nf.astype(BF16).astype(F32)
            mid = (nf - hi).astype(BF16).astype(F32)
            lo = (nf - hi) - mid
            aug = jnp.where(lane == 0, hi, jnp.where(lane == 1, mid, jnp.where(lane == 2, lo, 0.0)))
            kaug_s[rows, HEAD_DIM:2 * HEAD_DIM] = aug.astype(BF16)
            vaug_s[c, 0:HEAD_DIM, :] = v_ref[rows, :].T.astype(BF16)
            vaug_s[c, HEAD_DIM:V_ROWS, :] = ones_row.astype(BF16)

    q = q_ref[...]
    ms = jnp.mean(q * q, axis=-1, keepdims=True)
    qaug_s[:, 0:HEAD_DIM] = (q * lax.rsqrt(ms + EPS) * qn_ref[...]
                             * (HEAD_DIM ** -0.5 * LOG2E)).astype(BF16)
    qaug_s[:, HEAD_DIM:2 * HEAD_DIM] = (
        lax.broadcasted_iota(jnp.int32, (qblk, HEAD_DIM), 1) < 3).astype(BF16)
    m_s[...] = jnp.full(m_s.shape, MASK_VALUE, F32)
    acc_s[...] = jnp.zeros(acc_s.shape, F32)

    def scores(kj):
        rows = pl.ds(pl.multiple_of(kj * blk, blk), blk)
        return lax.dot_general(kaug_s[rows, :], qaug_s[...], _NT, preferred_element_type=F32)

    def update(s_ref, kj, key_offset):
        st = s_ref[...]
        if key_offset is not None:
            kr = lax.broadcasted_iota(jnp.int32, (blk, qblk), 0) + key_offset
            qc = lax.broadcasted_iota(jnp.int32, (blk, qblk), 1)
            st = jnp.where(kr <= qc, st, MASK_VALUE)
        m_old = m_s[...]
        m_new = jnp.maximum(m_old, jnp.max(st, axis=0, keepdims=True))
        pt = jnp.exp2(st - m_new).astype(BF16)
        alpha = jnp.exp2(m_old - m_new)
        acc_s[...] = alpha * acc_s[...] + jnp.dot(vaug_s[kj], pt, preferred_element_type=F32)
        m_s[...] = m_new

    s0_s[...] = scores(0)

    def pair(i, carry):
        kj = 2 * i
        s1_s[...] = scores(kj + 1)
        update(s0_s, kj, None)
        s0_s[...] = scores(kj + 2)
        update(s1_s, kj + 1, None)
        return carry

    lax.fori_loop(0, qi, pair, 0)
    s1_s[...] = scores(2 * qi + 1)
    update(s0_s, 2 * qi, 0)
    update(s1_s, 2 * qi + 1, blk)

    acc = acc_s[...]
    ot = acc[0:HEAD_DIM, :] / acc[HEAD_DIM:HEAD_DIM + 1, :]
    o_ref[...] = ot.T.astype(o_ref.dtype)


def _fox(proj3, fc, qn_w, kn_w, *, n_heads, q_blk0, k_blk0, v_blk0, blk):
    b, s, _ = proj3.shape
    n_blk = s // blk
    qblk = 2 * blk
    assert s % qblk == 0
    return pl.pallas_call(
        functools.partial(_fox_kernel, blk=blk, n_blk=n_blk),
        grid=(b, n_heads, s // qblk),
        in_specs=[
            pl.BlockSpec((None, qblk, HEAD_DIM), lambda i, h, t: (i, t, q_blk0 + h)),
            pl.BlockSpec((None, s, HEAD_DIM), lambda i, h, t: (i, 0, k_blk0 + h)),
            pl.BlockSpec((None, s, HEAD_DIM), lambda i, h, t: (i, 0, v_blk0 + h)),
            pl.BlockSpec((None, s, HEAD_DIM), lambda i, h, t: (i, 0, 0)),
            pl.BlockSpec((1, HEAD_DIM), lambda i, h, t: (0, 0)),
            pl.BlockSpec((1, HEAD_DIM), lambda i, h, t: (0, 0)),
        ],
        out_specs=pl.BlockSpec((None, qblk, HEAD_DIM), lambda i, h, t: (i, t, h)),
        out_shape=jax.ShapeDtypeStruct((b, s, n_heads * HEAD_DIM), BF16),
        scratch_shapes=[
            pltpu.VMEM((s, 2 * HEAD_DIM), BF16),
            pltpu.VMEM((n_blk, V_ROWS, blk), BF16),
            pltpu.VMEM((qblk, 2 * HEAD_DIM), BF16),
            pltpu.VMEM((blk, qblk), F32),
            pltpu.VMEM((blk, qblk), F32),
            pltpu.VMEM((1, qblk), F32),
            pltpu.VMEM((V_ROWS, qblk), F32),
        ],
        compiler_params=_params("arbitrary", "arbitrary", "arbitrary"),
        name="fox_attn",
    )(proj3, proj3, proj3, fc, qn_w.reshape(1, HEAD_DIM), kn_w.reshape(1, HEAD_DIM))


def _hgrn2_kernel(q_ref, f_ref, i_ref, g_ref, lb_ref, nw_ref, seg_ref, o_ref,
                  st_s, y_s, g_s, kk_s, qf_s, r_s, a_s, qe_s, u_s, dec_s, *, layer, rows):
    t_id = pl.program_id(2)
    n_chunk = rows // CHUNK

    @pl.when(t_id == 0)
    def _():
        st_s[...] = jnp.zeros(st_s.shape, F32)

    depth = lb_ref.shape[0]
    lbs = [lb_ref[j] for j in range(depth)]
    mx = lbs[0]
    for j in range(1, depth):
        mx = jnp.maximum(mx, lbs[j])
    es = [jnp.exp(v - mx) for v in lbs]
    den = es[0]
    for j in range(1, depth):
        den = den + es[j]
    lb = jnp.zeros((1, HEAD_DIM), F32)
    for j in range(1, layer + 1):
        lb = lb + es[j] / den

    tril = _tril(CHUNK)

    half = SUB // 2
    zeros_half = jnp.zeros((half, HEAD_DIM), F32)
    row_half = lax.broadcasted_iota(jnp.int32, (half, HEAD_DIM), 0)

    def prep(c, carry):
        rws = pl.ds(pl.multiple_of(c * CHUNK, CHUNK), CHUNK)
        z = f_ref[rws, :]
        kk = (1.0 - lb) * _sigmoid(-z)
        logf = jnp.log1p(-jnp.minimum(kk, K_MAX))
        gc = _tril_cumsum(tril, logf) * LOG2E
        q = q_ref[rws, :]
        qf = q * _sigmoid(q)
        kk_s[rws, :] = kk
        g_s[rws, :] = gc
        qf_s[rws, :] = qf
        for b in range(N_SUB):
            r0 = b * SUB
            q_t, q_b = qf[r0:r0 + half], qf[r0 + half:r0 + SUB]
            k_t, k_b = kk[r0:r0 + half], kk[r0 + half:r0 + SUB]
            g_t, g_b = gc[r0:r0 + half], gc[r0 + half:r0 + SUB]
            m_t = g_t[half // 2:half // 2 + 1]
            m_b = g_b[half // 2:half // 2 + 1]
            g_8 = g_b[0:1]
            qt_f = q_t * jnp.exp2(g_t - m_t)
            pt_f = k_t * jnp.exp2(m_t - g_t)
            qb_f = q_b * jnp.exp2(g_b - m_b)
            pb_f = k_b * jnp.exp2(m_b - g_b)
            qx_f = q_b * jnp.exp2(g_b - g_8)
            px_f = k_t * jnp.exp2(g_8 - g_t)
            yrows = pl.ds(pl.multiple_of(c * CHUNK + r0, SUB), SUB)
            for s in range(SUB):
                if s < half:
                    top = jnp.where(row_half >= s, qt_f * pt_f[s:s + 1], 0.0)
                    bot = qx_f * px_f[s:s + 1]
                else:
                    top = zeros_half
                    bot = jnp.where(row_half >= s - half, qb_f * pb_f[s - half:s - half + 1], 0.0)
                x = jnp.concatenate([top, bot], axis=0)
                y_s[yrows, s * HEAD_DIM:(s + 1) * HEAD_DIM] = x.astype(BF16)
        return carry

    lax.fori_loop(0, n_chunk, prep, 0, unroll=True)

    r_s[...] = jnp.dot(y_s[...], seg_ref[...], preferred_element_type=F32)

    tr = lax.broadcasted_iota(jnp.int32, (CHUNK, CHUNK), 0)
    tc = lax.broadcasted_iota(jnp.int32, (CHUNK, CHUNK), 1)
    same_blk = (tr // SUB) == (tc // SUB)
    diag_mask = jnp.logical_and(same_blk, tc <= tr)
    off_mask = (tc // SUB) < (tr // SUB)

    def scores(c, carry):
        rws = pl.ds(pl.multiple_of(c * CHUNK, CHUNK), CHUNK)
        qf = qf_s[rws, :]
        kk = kk_s[rws, :]
        gc = g_s[rws, :]
        blocks = [jnp.zeros((SUB, CHUNK), F32)]
        for i in range(1, N_SUB):
            gn = gc[i * SUB:i * SUB + 1]
            ql = qf[i * SUB:(i + 1) * SUB] * jnp.exp2(gc[i * SUB:(i + 1) * SUB] - gn)
            kr = kk * jnp.exp2(jnp.minimum(gn - gc, 0.0))
            blocks.append(lax.dot_general(ql.astype(BF16), kr.astype(BF16), _NT,
                                          preferred_element_type=F32))
        a_off = jnp.concatenate(blocks, axis=0)
        a = jnp.where(diag_mask, r_s[rws, :][:, :CHUNK], jnp.where(off_mask, a_off, 0.0))
        a_s[rws, :] = a.astype(BF16)
        g_last = gc[CHUNK - 1:CHUNK]
        qe_s[rws, :] = (qf * jnp.exp2(gc)).astype(BF16)
        kd = kk * jnp.exp2(g_last - gc)
        u_s[c] = lax.dot_general(i_ref[rws, :].astype(BF16), kd.astype(BF16), _TN,
                                 preferred_element_type=F32)
        dec_s[c] = jnp.exp2(g_last)
        return carry

    lax.fori_loop(0, n_chunk, scores, 0, unroll=True)

    def chunk(c, carry):
        rws = pl.ds(pl.multiple_of(c * CHUNK, CHUNK), CHUNK)
        vb = i_ref[rws, :].astype(BF16)
        st = st_s[...]
        o = jnp.dot(a_s[rws, :], vb, preferred_element_type=F32)
        o = o + lax.dot_general(qe_s[rws, :], st.astype(BF16), _NT,
                                preferred_element_type=F32)
        st_s[...] = dec_s[c] * st + u_s[c]
        ms = jnp.mean(o * o, axis=-1, keepdims=True)
        y = o * lax.rsqrt(ms + EPS) * nw_ref[...]
        g = g_ref[rws, :]
        o_ref[rws, :] = (y * (g * _sigmoid(g))).astype(o_ref.dtype)
        return carry

    lax.fori_loop(0, n_chunk, chunk, 0, unroll=True)


def _hgrn2(proj3, lower_bounds, nw, *, layer, n_heads, rows):
    b, s, _ = proj3.shape
    depth = lower_bounds.shape[0]
    rows = min(rows, s)
    seg_r = jnp.arange(SUB * HEAD_DIM, dtype=jnp.int32)[:, None] // HEAD_DIM
    seg_c = jnp.arange(HEAD_DIM, dtype=jnp.int32)[None, :]
    seg = jnp.logical_and(seg_c < CHUNK, (seg_c % SUB) == seg_r).astype(BF16)
    lb4 = lower_bounds.astype(F32).reshape(depth, n_heads, 1, HEAD_DIM)

    def col(off):
        return pl.BlockSpec((None, rows, HEAD_DIM), lambda i, h, t: (i, t, off * n_heads + h))

    return pl.pallas_call(
        functools.partial(_hgrn2_kernel, layer=layer, rows=rows),
        grid=(b, n_heads, s // rows),
        in_specs=[
            col(0), col(1), col(2), col(3),
            pl.BlockSpec((depth, None, 1, HEAD_DIM), lambda i, h, t: (0, h, 0, 0)),
            pl.BlockSpec((1, HEAD_DIM), lambda i, h, t: (0, 0)),
            pl.BlockSpec((SUB * HEAD_DIM, HEAD_DIM), lambda i, h, t: (0, 0)),
        ],
        out_specs=pl.BlockSpec((None, rows, HEAD_DIM), lambda i, h, t: (i, t, h)),
        out_shape=jax.ShapeDtypeStruct((b, s, n_heads * HEAD_DIM), BF16),
        scratch_shapes=[
            pltpu.VMEM((HEAD_DIM, HEAD_DIM), F32),
            pltpu.VMEM((rows, SUB * HEAD_DIM), BF16),
            pltpu.VMEM((rows, HEAD_DIM), F32),
            pltpu.VMEM((rows, HEAD_DIM), F32),
            pltpu.VMEM((rows, HEAD_DIM), F32),
            pltpu.VMEM((rows, HEAD_DIM), F32),
            pltpu.VMEM((rows, CHUNK), BF16),
            pltpu.VMEM((rows, HEAD_DIM), BF16),
            pltpu.VMEM((rows // CHUNK, HEAD_DIM, HEAD_DIM), F32),
            pltpu.VMEM((rows // CHUNK, 1, HEAD_DIM), F32),
        ],
        compiler_params=_params("arbitrary", "arbitrary", "arbitrary"),
        name="hgrn2",
    )(proj3, proj3, proj3, proj3, lb4, nw.reshape(1, HEAD_DIM), seg)


def kernel(x, c, lower_bounds, w_ada, b_ada, norm_mix_w, norm_ffn_w, w_in, rec_norm_w,
           fg_bias, q_norm_w, k_norm_w, w_out, w_up, w_down):
    b, s, d = x.shape
    depth = w_ada.shape[0]
    n_heads = d // HEAD_DIM
    h_rec = n_heads // 2
    h_att = n_heads - h_rec
    main_cols = (4 * h_rec + 3 * h_att) * HEAD_DIM
    assert w_in.shape[2] == main_cols + h_att and h_att <= HEAD_DIM
    assert s % CHUNK == 0 and s % HEAD_DIM == 0
    m = b * s
    fox_blk = min(512, s // 2)

    rows_pad = -(-b // 8) * 8
    c_pad = jnp.zeros((rows_pad, d), F32).at[:b].set(c.astype(F32))
    mod = _adaln_mod(c_pad, w_ada, b_ada)

    w_down_b = w_down.astype(BF16)
    w_in_t = jnp.swapaxes(w_in, 1, 2)

    x2 = x.reshape(m, d)
    for l in range(depth):
        mod6 = mod[l, :b].reshape(b * N_MOD, 1, d)

        h = _norm_mod(x2.reshape(b, s, d), norm_mix_w[l], mod6, 1, 0).reshape(m, d)
        proj = _matmul_wcast([h], w_in_t, l, mode="plain", out_dtype=F32, tm=1024, tn=1024,
                             n_cols=main_cols, w_t=True)
        fg = _fg_proj(h, w_in_t, l, row0=main_cols, tm=1024)
        proj3 = proj.reshape(b, s, main_cols)

        o_rec = _hgrn2(proj3, lower_bounds, rec_norm_w[l], layer=l, n_heads=h_rec, rows=1024)

        bias_row = jnp.pad(fg_bias[l].astype(F32), (0, HEAD_DIM - h_att)).reshape(1, HEAD_DIM)
        fc = _fcum(fg.reshape(b, s, HEAD_DIM), bias_row)
        o_att = _fox(proj3, fc, q_norm_w[l], k_norm_w[l], n_heads=h_att,
                     q_blk0=4 * h_rec, k_blk0=4 * h_rec + h_att, v_blk0=4 * h_rec + 2 * h_att,
                     blk=fox_blk)

        x2 = _matmul_wcast([o_rec.reshape(m, -1), o_att.reshape(m, -1)], w_out, l, mode="resid",
                           out_dtype=F32, tm=1024, tn=512, resid=(x2, mod6, 2),
                           rows_per_batch=s)

        h2 = _norm_mod(x2.reshape(b, s, d), norm_ffn_w[l], mod6, 4, 3).reshape(m, d)
        u = _matmul_wcast([h2], w_up, l, mode="relu2", out_dtype=BF16, tm=1024, tn=1024)
        x2 = _matmul([u], w_down_b, l, mode="resid", out_dtype=F32, tm=1024, tn=1024, tk=4096,
                     resid=(x2, mod6, 5), rows_per_batch=s)
    return x2.reshape(b, s, d)
```

```python
import functools

import jax
import jax.numpy as jnp
from jax import lax
from jax.experimental import pallas as pl
from jax.experimental.pallas import tpu as pltpu

HEAD_DIM = 128
CHUNK = 64
SUB = 16
N_SUB = CHUNK // SUB
N_MOD = 6
EPS = 1e-6
MASK_VALUE = -1e30
K_MAX = 1.0 - 1e-6
LOG2E = 1.4426950408889634
VMEM_LIMIT_BYTES = 60 * 1024 * 1024

F32 = jnp.float32
BF16 = jnp.bfloat16

_NT = (((1,), (1,)), ((), ()))
_TN = (((0,), (0,)), ((), ()))


def _params(*sem):
    return pltpu.CompilerParams(dimension_semantics=sem, vmem_limit_bytes=VMEM_LIMIT_BYTES)


def _fit_tile(tile, dim):
    tile = min(tile, dim)
    while dim % tile:
        tile -= HEAD_DIM
    return tile


def _sigmoid(x):
    return 1.0 / (1.0 + jnp.exp(-x))


def _tril_cumsum(tril, x):
    hi = x.astype(BF16)
    r1 = x - hi.astype(F32)
    mid = r1.astype(BF16)
    lo = (r1 - mid.astype(F32)).astype(BF16)
    out = jnp.dot(tril, hi, preferred_element_type=F32)
    out = out + jnp.dot(tril, mid, preferred_element_type=F32)
    return out + jnp.dot(tril, lo, preferred_element_type=F32)


def _tril(n):
    r = lax.broadcasted_iota(jnp.int32, (n, n), 0)
    c = lax.broadcasted_iota(jnp.int32, (n, n), 1)
    return (r >= c).astype(BF16)


def _mod_kernel(c_ref, w_ref, b_ref, o_ref):
    c = c_ref[...]
    ca = (c * _sigmoid(c)).astype(BF16)
    w = w_ref[...].astype(BF16)
    o_ref[...] = jnp.dot(ca, w, preferred_element_type=F32) + b_ref[...]


def _adaln_mod(c_pad, w_ada, b_ada):
    depth, d, n = w_ada.shape
    rows = c_pad.shape[0]
    tn = min(512, n)
    return pl.pallas_call(
        _mod_kernel,
        grid=(depth, n // tn),
        in_specs=[
            pl.BlockSpec((rows, d), lambda l, j: (0, 0)),
            pl.BlockSpec((None, d, tn), lambda l, j: (l, 0, j)),
            pl.BlockSpec((None, 1, tn), lambda l, j: (l, 0, j)),
        ],
        out_specs=pl.BlockSpec((None, rows, tn), lambda l, j: (l, 0, j)),
        out_shape=jax.ShapeDtypeStruct((depth, rows, n), F32),
        compiler_params=_params("arbitrary", "arbitrary"),
        name="adaln_mod",
    )(c_pad, w_ada, b_ada.reshape(depth, 1, n))


def _norm_mod_kernel(x_ref, w_ref, scale_ref, shift_ref, *rest, n_valid):
    x = x_ref[...]
    ms = jnp.mean(x * x, axis=-1, keepdims=True)
    y = x * lax.rsqrt(ms + EPS) * w_ref[...]
    h = (y * (1.0 + scale_ref[...]) + shift_ref[...]).astype(BF16)
    if n_valid is None:
        o_ref, = rest
    else:
        wfg_ref, o_ref, fg_ref = rest
        row = lax.broadcasted_iota(jnp.int32, wfg_ref.shape, 0)
        wfg = jnp.where(row < n_valid, wfg_ref[...], 0.0).astype(BF16)
        fg_ref[...] = lax.dot_general(h, wfg, _NT, preferred_element_type=F32)
    o_ref[...] = h


def _norm_mod(x3, w, mod6, scale_idx, shift_idx, fg_weights=None):
    b, s, d = x3.shape
    ts = min(512, s)
    in_specs = [
        pl.BlockSpec((None, ts, d), lambda i, t: (i, t, 0)),
        pl.BlockSpec((1, d), lambda i, t: (0, 0)),
        pl.BlockSpec((None, 1, d), lambda i, t: (i * N_MOD + scale_idx, 0, 0)),
        pl.BlockSpec((None, 1, d), lambda i, t: (i * N_MOD + shift_idx, 0, 0)),
    ]
    args = [x3, w.reshape(1, d), mod6, mod6]
    out_specs = [pl.BlockSpec((None, ts, d), lambda i, t: (i, t, 0))]
    out_shape = [jax.ShapeDtypeStruct((b, s, d), BF16)]
    n_valid = None
    if fg_weights is not None:
        w_in_t, layer, row0 = fg_weights
        n_valid = w_in_t.shape[1] - row0
        assert row0 % HEAD_DIM == 0 and 0 < n_valid <= HEAD_DIM and w_in_t.shape[2] == d
        in_specs.append(pl.BlockSpec((None, HEAD_DIM, d),
                                     lambda i, t: (layer, row0 // HEAD_DIM, 0)))
        args.append(w_in_t)
        out_specs.append(pl.BlockSpec((None, ts, HEAD_DIM), lambda i, t: (i, t, 0)))
        out_shape.append(jax.ShapeDtypeStruct((b, s, HEAD_DIM), F32))
    outs = pl.pallas_call(
        functools.partial(_norm_mod_kernel, n_valid=n_valid),
        grid=(b, s // ts),
        in_specs=in_specs,
        out_specs=out_specs,
        out_shape=out_shape,
        compiler_params=_params("arbitrary", "arbitrary"),
        name="norm_mod",
    )(*args)
    return outs if fg_weights is not None else outs[0]


def _slab_dots(a_refs, w_ref, w_t=False):
    part = None
    off = 0
    for a_ref in a_refs:
        ka = a_ref.shape[1]
        if w_t:
            p = lax.dot_general(a_ref[...], w_ref[:, off:off + ka], _NT,
                                preferred_element_type=F32)
        else:
            p = jnp.dot(a_ref[...], w_ref[off:off + ka, :], preferred_element_type=F32)
        part = p if part is None else part + p
        off += ka
    return part


def _mm_epilogue(mode, acc, o_ref, x_ref, g_ref):
    if mode == "plain":
        o_ref[...] = acc.astype(o_ref.dtype)
    elif mode == "relu2":
        r = jnp.maximum(acc, 0.0)
        o_ref[...] = (r * r).astype(o_ref.dtype)
    else:
        o_ref[...] = x_ref[...] + g_ref[...] * acc


W_CAST_ROWS = 512


def _mm_wcast_kernel(*refs, n_a, mode, layer, n_col_tiles, tn, w_t, side):
    a_refs = refs[:n_a]
    w_hbm = refs[n_a]
    rest = list(refs[n_a + 1:])
    x_ref, g_ref = (rest.pop(0), rest.pop(0)) if mode == "resid" else (None, None)
    side_in = rest.pop(0) if side else None
    o_ref = rest.pop(0)
    side_out = rest.pop(0) if side else None
    wf_s, wb_s, sem = rest
    j = pl.program_id(0)

    if side:
        side_out[...] = side_in[...].astype(side_out.dtype)

    def tile_copy(jj):
        sel = pl.ds(pl.multiple_of(jj * tn, tn), tn)
        src = w_hbm.at[layer, sel, :] if w_t else w_hbm.at[layer, :, sel]
        return pltpu.make_async_copy(src, wf_s, sem)

    @pl.when(pl.program_id(1) == 0)
    def _():
        @pl.when(j == 0)
        def _():
            tile_copy(0).start()

        tile_copy(j).wait()
        n_rows = wf_s.shape[0]
        for r0 in range(0, n_rows, W_CAST_ROWS):
            r1 = min(r0 + W_CAST_ROWS, n_rows)
            wb_s[r0:r1, :] = wf_s[r0:r1, :].astype(BF16)

        @pl.when(j + 1 < n_col_tiles)
        def _():
            tile_copy(j + 1).start()

    _mm_epilogue(mode, _slab_dots(a_refs, wb_s, w_t), o_ref, x_ref, g_ref)


def _matmul_wcast(a_list, w_stack, layer, *, mode, out_dtype, tm, tn, n_cols=None, w_t=False,
                  resid=None, rows_per_batch=None, side_cast=None):
    m = a_list[0].shape[0]
    if w_t:
        _, n_total, kdim = w_stack.shape
    else:
        _, kdim, n_total = w_stack.shape
    n = n_total if n_cols is None else n_cols
    n_a = len(a_list)
    tm = _fit_tile(tm, m)
    tn = _fit_tile(tn, n)
    w_tile = (tn, kdim) if w_t else (kdim, tn)
    assert m % tm == 0 and n % tn == 0
    assert sum(a.shape[1] for a in a_list) == kdim

    in_specs = [pl.BlockSpec((tm, a.shape[1]), lambda j, i: (i, 0)) for a in a_list]
    in_specs.append(pl.BlockSpec(memory_space=pl.ANY))
    args = list(a_list) + [w_stack]
    if mode == "resid":
        x2, mod6, gate_idx = resid
        assert rows_per_batch % tm == 0
        tiles_per_batch = rows_per_batch // tm
        in_specs.append(pl.BlockSpec((tm, tn), lambda j, i: (i, j)))
        in_specs.append(pl.BlockSpec(
            (None, 1, tn), lambda j, i: ((i // tiles_per_batch) * N_MOD + gate_idx, 0, j)))
        args += [x2, mod6]
    out_specs = [pl.BlockSpec((tm, tn), lambda j, i: (i, j))]
    out_shape = [jax.ShapeDtypeStruct((m, n), out_dtype)]
    n_i = m // tm
    if side_cast is not None:
        side_stack, side_layer = side_cast
        _, s_rows, s_cols = side_stack.shape
        n_steps = (n // tn) * n_i
        rs = s_rows // n_steps
        assert rs * n_steps == s_rows and rs % 16 == 0
        in_specs.append(pl.BlockSpec((None, rs, s_cols), lambda j, i: (side_layer, j * n_i + i, 0)))
        args.append(side_stack)
        out_specs.append(pl.BlockSpec((rs, s_cols), lambda j, i: (j * n_i + i, 0)))
        out_shape.append(jax.ShapeDtypeStruct((s_rows, s_cols), BF16))
    outs = pl.pallas_call(
        functools.partial(_mm_wcast_kernel, n_a=n_a, mode=mode, layer=layer,
                          n_col_tiles=n // tn, tn=tn, w_t=w_t, side=side_cast is not None),
        grid=(n // tn, n_i),
        in_specs=in_specs,
        out_specs=out_specs,
        out_shape=out_shape,
        scratch_shapes=[pltpu.VMEM(w_tile, F32), pltpu.VMEM(w_tile, BF16),
                        pltpu.SemaphoreType.DMA(())],
        compiler_params=_params("arbitrary", "arbitrary"),
        name="mmw_" + mode,
    )(*args)
    return outs if side_cast is not None else outs[0]


def _mm_kernel(*refs, n_a, nk, mode):
    a_refs = refs[:n_a]
    w_ref = refs[n_a]
    if mode == "resid":
        x_ref, g_ref, o_ref = refs[n_a + 1:]
    else:
        o_ref, = refs[n_a + 1:]
        x_ref = g_ref = None

    part = _slab_dots(a_refs, w_ref)

    def finish(acc):
        _mm_epilogue(mode, acc, o_ref, x_ref, g_ref)

    if nk == 1:
        finish(part)
    else:
        k = pl.program_id(2)

        @pl.when(k == 0)
        def _():
            o_ref[...] = part

        @pl.when(jnp.logical_and(k > 0, k < nk - 1))
        def _():
            o_ref[...] += part

        @pl.when(k == nk - 1)
        def _():
            finish(o_ref[...] + part)


def _matmul(a_list, w_stack, layer, *, mode, out_dtype, tm, tn, tk=None, n_cols=None,
            resid=None, rows_per_batch=None):
    m = a_list[0].shape[0]
    _, kdim, n_total = w_stack.shape
    n = n_total if n_cols is None else n_cols
    n_a = len(a_list)
    tm = _fit_tile(tm, m)
    tn = _fit_tile(tn, n)
    if tk is None or tk >= kdim:
        tk = kdim
    nk = kdim // tk
    assert n_a == 1 or nk == 1
    assert nk == 1 or out_dtype == F32
    assert m % tm == 0 and n % tn == 0 and kdim % tk == 0

    in_specs = []
    if n_a == 1:
        in_specs.append(pl.BlockSpec((tm, tk), lambda i, j, k: (i, k)))
    else:
        for a in a_list:
            in_specs.append(pl.BlockSpec((tm, a.shape[1]), lambda i, j, k: (i, 0)))
    in_specs.append(pl.BlockSpec((None, tk, tn), lambda i, j, k: (layer, k, j)))
    args = list(a_list) + [w_stack]
    if mode == "resid":
        x2, mod6, gate_idx = resid
        assert rows_per_batch % tm == 0
        tiles_per_batch = rows_per_batch // tm
        in_specs.append(pl.BlockSpec((tm, tn), lambda i, j, k: (i, j)))
        in_specs.append(pl.BlockSpec(
            (None, 1, tn), lambda i, j, k: ((i // tiles_per_batch) * N_MOD + gate_idx, 0, j)))
        args += [x2, mod6]
    return pl.pallas_call(
        functools.partial(_mm_kernel, n_a=n_a, nk=nk, mode=mode),
        grid=(m // tm, n // tn, nk),
        in_specs=in_specs,
        out_specs=pl.BlockSpec((tm, tn), lambda i, j, k: (i, j)),
        out_shape=jax.ShapeDtypeStruct((m, n), out_dtype),
        compiler_params=_params("arbitrary", "arbitrary", "arbitrary"),
        name="mm_" + mode,
    )(*args)


def _fcum_kernel(fg_ref, bias_ref, fc_ref):
    n_slab = fg_ref.shape[0] // HEAD_DIM
    tril = _tril(HEAD_DIM)

    def body(r, carry):
        rows = pl.ds(pl.multiple_of(r * HEAD_DIM, HEAD_DIM), HEAD_DIM)
        z = fg_ref[rows, :] + bias_ref[...]
        lf = jnp.minimum(z, 0.0) - jnp.log1p(jnp.exp(-jnp.abs(z)))
        f = _tril_cumsum(tril, lf) + carry
        fc_ref[rows, :] = f
        return f[HEAD_DIM - 1:HEAD_DIM, :]

    lax.fori_loop(0, n_slab, body, jnp.zeros((1, HEAD_DIM), F32))


def _fcum(fg3, bias_row):
    b, s, _ = fg3.shape
    return pl.pallas_call(
        _fcum_kernel,
        grid=(b,),
        in_specs=[
            pl.BlockSpec((None, s, HEAD_DIM), lambda i: (i, 0, 0)),
            pl.BlockSpec((1, HEAD_DIM), lambda i: (0, 0)),
        ],
        out_specs=pl.BlockSpec((None, s, HEAD_DIM), lambda i: (i, 0, 0)),
        out_shape=jax.ShapeDtypeStruct((b, s, HEAD_DIM), F32),
        compiler_params=_params("arbitrary"),
        name="fox_fcum",
    )(fg3, bias_row)


V_ROWS = HEAD_DIM + 16


def _fox_kernel(q_ref, k_ref, v_ref, fc_ref, qn_ref, kn_ref, o_ref,
                kaug_s, vaug_s, qaug_s, s0_s, s1_s, m_s, acc_s, *, blk, n_blk):
    h = pl.program_id(1)
    qi = pl.program_id(2)
    qblk = 2 * blk
    lane = lax.broadcasted_iota(jnp.int32, (blk, HEAD_DIM), 1)

    @pl.when(qi == 0)
    def _():
        ones_row = (lax.broadcasted_iota(jnp.int32, (V_ROWS - HEAD_DIM, blk), 0) == 0)
        for c in range(n_blk):
            rows = slice(c * blk, (c + 1) * blk)
            k = k_ref[rows, :]
            ms = jnp.mean(k * k, axis=-1, keepdims=True)
            kaug_s[rows, 0:HEAD_DIM] = (k * lax.rsqrt(ms + EPS) * kn_ref[...]).astype(BF16)
            fcol = jnp.sum(jnp.where(lane == h, fc_ref[rows, :], 0.0), axis=1, keepdims=True)
            nf = fcol * (-LOG2E)
            hi = nf.astype(BF16).astype(F32)
            mid = (nf - hi).astype(BF16).astype(F32)
            lo = (nf - hi) - mid
            aug = jnp.where(lane == 0, hi, jnp.where(lane == 1, mid, jnp.where(lane == 2, lo, 0.0)))
            kaug_s[rows, HEAD_DIM:2 * HEAD_DIM] = aug.astype(BF16)
            vaug_s[c, 0:HEAD_DIM, :] = v_ref[rows, :].T.astype(BF16)
            vaug_s[c, HEAD_DIM:V_ROWS, :] = ones_row.astype(BF16)

    q = q_ref[...]
    ms = jnp.mean(q * q, axis=-1, keepdims=True)
    qaug_s[:, 0:HEAD_DIM] = (q * lax.rsqrt(ms + EPS) * qn_ref[...]
                             * (HEAD_DIM ** -0.5 * LOG2E)).astype(BF16)
    qaug_s[:, HEAD_DIM:2 * HEAD_DIM] = (
        lax.broadcasted_iota(jnp.int32, (qblk, HEAD_DIM), 1) < 3).astype(BF16)
    m_s[...] = jnp.full(m_s.shape, MASK_VALUE, F32)
    acc_s[...] = jnp.zeros(acc_s.shape, F32)

    def scores(kj):
        rows = pl.ds(pl.multiple_of(kj * blk, blk), blk)
        return lax.dot_general(kaug_s[rows, :], qaug_s[...], _NT, preferred_element_type=F32)

    def update(s_ref, kj, key_offset):
        st = s_ref[...]
        if key_offset is not None:
            kr = lax.broadcasted_iota(jnp.int32, (blk, qblk), 0) + key_offset
            qc = lax.broadcasted_iota(jnp.int32, (blk, qblk), 1)
            st = jnp.where(kr <= qc, st, MASK_VALUE)
        m_old = m_s[...]
        m_new = jnp.maximum(m_old, jnp.max(st, axis=0, keepdims=True))
        pt = jnp.exp2(st - m_new).astype(BF16)
        alpha = jnp.exp2(m_old - m_new)
        acc_s[...] = alpha * acc_s[...] + jnp.dot(vaug_s[kj], pt, preferred_element_type=F32)
        m_s[...] = m_new

    s0_s[...] = scores(0)

    def pair(i, carry):
        kj = 2 * i
        s1_s[...] = scores(kj + 1)
        update(s0_s, kj, None)
        s0_s[...] = scores(kj + 2)
        update(s1_s, kj + 1, None)
        return carry

    lax.fori_loop(0, qi, pair, 0)
    s1_s[...] = scores(2 * qi + 1)
    update(s0_s, 2 * qi, 0)
    update(s1_s, 2 * qi + 1, blk)

    acc = acc_s[...]
    ot = acc[0:HEAD_DIM, :] / acc[HEAD_DIM:HEAD_DIM + 1, :]
    o_ref[...] = ot.T.astype(o_ref.dtype)


def _fox(proj3, fc, qn_w, kn_w, *, n_heads, q_blk0, k_blk0, v_blk0, blk):
    b, s, _ = proj3.shape
    n_blk = s // blk
    qblk = 2 * blk
    assert s % qblk == 0
    return pl.pallas_call(
        functools.partial(_fox_kernel, blk=blk, n_blk=n_blk),
        grid=(b, n_heads, s // qblk),
        in_specs=[
            pl.BlockSpec((None, qblk, HEAD_DIM), lambda i, h, t: (i, t, q_blk0 + h)),
            pl.BlockSpec((None, s, HEAD_DIM), lambda i, h, t: (i, 0, k_blk0 + h)),
            pl.BlockSpec((None, s, HEAD_DIM), lambda i, h, t: (i, 0, v_blk0 + h)),
            pl.BlockSpec((None, s, HEAD_DIM), lambda i, h, t: (i, 0, 0)),
            pl.BlockSpec((1, HEAD_DIM), lambda i, h, t: (0, 0)),
            pl.BlockSpec((1, HEAD_DIM), lambda i, h, t: (0, 0)),
        ],
        out_specs=pl.BlockSpec((None, qblk, HEAD_DIM), lambda i, h, t: (i, t, h)),
        out_shape=jax.ShapeDtypeStruct((b, s, n_heads * HEAD_DIM), BF16),
        scratch_shapes=[
            pltpu.VMEM((s, 2 * HEAD_DIM), BF16),
            pltpu.VMEM((n_blk, V_ROWS, blk), BF16),
            pltpu.VMEM((qblk, 2 * HEAD_DIM), BF16),
            pltpu.VMEM((blk, qblk), F32),
            pltpu.VMEM((blk, qblk), F32),
            pltpu.VMEM((1, qblk), F32),
            pltpu.VMEM((V_ROWS, qblk), F32),
        ],
        compiler_params=_params("arbitrary", "arbitrary", "arbitrary"),
        name="fox_attn",
    )(proj3, proj3, proj3, fc, qn_w.reshape(1, HEAD_DIM), kn_w.reshape(1, HEAD_DIM))


def _hgrn2_kernel(q_ref, f_ref, i_ref, g_ref, lb_ref, nw_ref, seg_ref, o_ref,
                  st_s, y_s, g_s, kk_s, qf_s, r_s, a_s, qe_s, u_s, dec_s, *, layer, rows):
    t_id = pl.program_id(2)
    n_chunk = rows // CHUNK

    @pl.when(t_id == 0)
    def _():
        st_s[...] = jnp.zeros(st_s.shape, F32)

    depth = lb_ref.shape[0]
    lbs = [lb_ref[j] for j in range(depth)]
    mx = lbs[0]
    for j in range(1, depth):
        mx = jnp.maximum(mx, lbs[j])
    es = [jnp.exp(v - mx) for v in lbs]
    den = es[0]
    for j in range(1, depth):
        den = den + es[j]
    lb = jnp.zeros((1, HEAD_DIM), F32)
    for j in range(1, layer + 1):
        lb = lb + es[j] / den

    tril = _tril(CHUNK)

    half = SUB // 2
    zeros_half = jnp.zeros((half, HEAD_DIM), F32)
    row_half = lax.broadcasted_iota(jnp.int32, (half, HEAD_DIM), 0)

    def prep(c, carry):
        rws = pl.ds(pl.multiple_of(c * CHUNK, CHUNK), CHUNK)
        z = f_ref[rws, :]
        kk = (1.0 - lb) * _sigmoid(-z)
        logf = jnp.log1p(-jnp.minimum(kk, K_MAX))
        gc = _tril_cumsum(tril, logf) * LOG2E
        q = q_ref[rws, :]
        qf = q * _sigmoid(q)
        kk_s[rws, :] = kk
        g_s[rws, :] = gc
        qf_s[rws, :] = qf
        for b in range(N_SUB):
            r0 = b * SUB
            q_t, q_b = qf[r0:r0 + half], qf[r0 + half:r0 + SUB]
            k_t, k_b = kk[r0:r0 + half], kk[r0 + half:r0 + SUB]
            g_t, g_b = gc[r0:r0 + half], gc[r0 + half:r0 + SUB]
            m_t = g_t[half // 2:half // 2 + 1]
            m_b = g_b[half // 2:half // 2 + 1]
            g_8 = g_b[0:1]
            qt_f = q_t * jnp.exp2(g_t - m_t)
            pt_f = k_t * jnp.exp2(m_t - g_t)
            qb_f = q_b * jnp.exp2(g_b - m_b)
            pb_f = k_b * jnp.exp2(m_b - g_b)
            qx_f = q_b * jnp.exp2(g_b - g_8)
            px_f = k_t * jnp.exp2(g_8 - g_t)
            yrows = pl.ds(pl.multiple_of(c * CHUNK + r0, SUB), SUB)
            for s in range(SUB):
                if s < half:
                    top = jnp.where(row_half >= s, qt_f * pt_f[s:s + 1], 0.0)
                    bot = qx_f * px_f[s:s + 1]
                else:
                    top = zeros_half
                    bot = jnp.where(row_half >= s - half, qb_f * pb_f[s - half:s - half + 1], 0.0)
                x = jnp.concatenate([top, bot], axis=0)
                y_s[yrows, s * HEAD_DIM:(s + 1) * HEAD_DIM] = x.astype(BF16)
        return carry

    lax.fori_loop(0, n_chunk, prep, 0, unroll=True)

    r_s[...] = jnp.dot(y_s[...], seg_ref[...], preferred_element_type=F32)

    tr = lax.broadcasted_iota(jnp.int32, (CHUNK, CHUNK), 0)
    tc = lax.broadcasted_iota(jnp.int32, (CHUNK, CHUNK), 1)
    same_blk = (tr // SUB) == (tc // SUB)
    diag_mask = jnp.logical_and(same_blk, tc <= tr)
    off_mask = (tc // SUB) < (tr // SUB)

    def scores(c, carry):
        rws = pl.ds(pl.multiple_of(c * CHUNK, CHUNK), CHUNK)
        qf = qf_s[rws, :]
        kk = kk_s[rws, :]
        gc = g_s[rws, :]
        blocks = [jnp.zeros((SUB, CHUNK), F32)]
        for i in range(1, N_SUB):
            gn = gc[i * SUB:i * SUB + 1]
            ql = qf[i * SUB:(i + 1) * SUB] * jnp.exp2(gc[i * SUB:(i + 1) * SUB] - gn)
            kr = kk * jnp.exp2(jnp.minimum(gn - gc, 0.0))
            blocks.append(lax.dot_general(ql.astype(BF16), kr.astype(BF16), _NT,
                                          preferred_element_type=F32))
        a_off = jnp.concatenate(blocks, axis=0)
        a = jnp.where(diag_mask, r_s[rws, :][:, :CHUNK], jnp.where(off_mask, a_off, 0.0))
        a_s[rws, :] = a.astype(BF16)
        g_last = gc[CHUNK - 1:CHUNK]
        qe_s[rws, :] = (qf * jnp.exp2(gc)).astype(BF16)
        kd = kk * jnp.exp2(g_last - gc)
        u_s[c] = lax.dot_general(i_ref[rws, :].astype(BF16), kd.astype(BF16), _TN,
                                 preferred_element_type=F32)
        dec_s[c] = jnp.exp2(g_last)
        return carry

    lax.fori_loop(0, n_chunk, scores, 0, unroll=True)

    def chunk(c, carry):
        rws = pl.ds(pl.multiple_of(c * CHUNK, CHUNK), CHUNK)
        vb = i_ref[rws, :].astype(BF16)
        st = st_s[...]
        o = jnp.dot(a_s[rws, :], vb, preferred_element_type=F32)
        o = o + lax.dot_general(qe_s[rws, :], st.astype(BF16), _NT,
                                preferred_element_type=F32)
        st_s[...] = dec_s[c] * st + u_s[c]
        ms = jnp.mean(o * o, axis=-1, keepdims=True)
        y = o * lax.rsqrt(ms + EPS) * nw_ref[...]
        g = g_ref[rws, :]
        o_ref[rws, :] = (y * (g * _sigmoid(g))).astype(o_ref.dtype)
        return carry

    lax.fori_loop(0, n_chunk, chunk, 0, unroll=True)


def _hgrn2(proj3, lower_bounds, nw, *, layer, n_heads, rows):
    b, s, _ = proj3.shape
    depth = lower_bounds.shape[0]
    rows = min(rows, s)
    seg_r = jnp.arange(SUB * HEAD_DIM, dtype=jnp.int32)[:, None] // HEAD_DIM
    seg_c = jnp.arange(HEAD_DIM, dtype=jnp.int32)[None, :]
    seg = jnp.logical_and(seg_c < CHUNK, (seg_c % SUB) == seg_r).astype(BF16)
    lb4 = lower_bounds.astype(F32).reshape(depth, n_heads, 1, HEAD_DIM)

    def col(off):
        return pl.BlockSpec((None, rows, HEAD_DIM), lambda i, h, t: (i, t, off * n_heads + h))

    return pl.pallas_call(
        functools.partial(_hgrn2_kernel, layer=layer, rows=rows),
        grid=(b, n_heads, s // rows),
        in_specs=[
            col(0), col(1), col(2), col(3),
            pl.BlockSpec((depth, None, 1, HEAD_DIM), lambda i, h, t: (0, h, 0, 0)),
            pl.BlockSpec((1, HEAD_DIM), lambda i, h, t: (0, 0)),
            pl.BlockSpec((SUB * HEAD_DIM, HEAD_DIM), lambda i, h, t: (0, 0)),
        ],
        out_specs=pl.BlockSpec((None, rows, HEAD_DIM), lambda i, h, t: (i, t, h)),
        out_shape=jax.ShapeDtypeStruct((b, s, n_heads * HEAD_DIM), BF16),
        scratch_shapes=[
            pltpu.VMEM((HEAD_DIM, HEAD_DIM), F32),
            pltpu.VMEM((rows, SUB * HEAD_DIM), BF16),
            pltpu.VMEM((rows, HEAD_DIM), F32),
            pltpu.VMEM((rows, HEAD_DIM), F32),
            pltpu.VMEM((rows, HEAD_DIM), F32),
            pltpu.VMEM((rows, HEAD_DIM), F32),
            pltpu.VMEM((rows, CHUNK), BF16),
            pltpu.VMEM((rows, HEAD_DIM), BF16),
            pltpu.VMEM((rows // CHUNK, HEAD_DIM, HEAD_DIM), F32),
            pltpu.VMEM((rows // CHUNK, 1, HEAD_DIM), F32),
        ],
        compiler_params=_params("arbitrary", "arbitrary", "arbitrary"),
        name="hgrn2",
    )(proj3, proj3, proj3, proj3, lb4, nw.reshape(1, HEAD_DIM), seg)


def kernel(x, c, lower_bounds, w_ada, b_ada, norm_mix_w, norm_ffn_w, w_in, rec_norm_w,
           fg_bias, q_norm_w, k_norm_w, w_out, w_up, w_down):
    b, s, d = x.shape
    depth = w_ada.shape[0]
    n_heads = d // HEAD_DIM
    h_rec = n_heads // 2
    h_att = n_heads - h_rec
    main_cols = (4 * h_rec + 3 * h_att) * HEAD_DIM
    assert w_in.shape[2] == main_cols + h_att and h_att <= HEAD_DIM
    assert s % CHUNK == 0 and s % HEAD_DIM == 0
    m = b * s
    fox_blk = min(512, s // 2)

    rows_pad = -(-b // 8) * 8
    c_pad = jnp.zeros((rows_pad, d), F32).at[:b].set(c.astype(F32))
    mod = _adaln_mod(c_pad, w_ada, b_ada)

    w_in_t = jnp.swapaxes(w_in, 1, 2)

    x2 = x.reshape(m, d)
    for l in range(depth):
        mod6 = mod[l, :b].reshape(b * N_MOD, 1, d)

        h, fg3 = _norm_mod(x2.reshape(b, s, d), norm_mix_w[l], mod6, 1, 0,
                           fg_weights=(w_in_t, l, main_cols))
        proj = _matmul_wcast([h.reshape(m, d)], w_in_t, l, mode="plain", out_dtype=F32,
                             tm=1024, tn=1024, n_cols=main_cols, w_t=True)
        proj3 = proj.reshape(b, s, main_cols)

        o_rec = _hgrn2(proj3, lower_bounds, rec_norm_w[l], layer=l, n_heads=h_rec, rows=1024)

        bias_row = jnp.pad(fg_bias[l].astype(F32), (0, HEAD_DIM - h_att)).reshape(1, HEAD_DIM)
        fc = _fcum(fg3, bias_row)
        o_att = _fox(proj3, fc, q_norm_w[l], k_norm_w[l], n_heads=h_att,
                     q_blk0=4 * h_rec, k_blk0=4 * h_rec + h_att, v_blk0=4 * h_rec + 2 * h_att,
                     blk=fox_blk)

        x2 = _matmul_wcast([o_rec.reshape(m, -1), o_att.reshape(m, -1)], w_out, l, mode="resid",
                           out_dtype=F32, tm=1024, tn=512, resid=(x2, mod6, 2),
                           rows_per_batch=s)

        h2 = _norm_mod(x2.reshape(b, s, d), norm_ffn_w[l], mod6, 4, 3).reshape(m, d)
        u, w_down_b = _matmul_wcast([h2], w_up, l, mode="relu2", out_dtype=BF16, tm=1024,
                                    tn=1024, side_cast=(w_down, l))
        x2 = _matmul([u], w_down_b[None], 0, mode="resid", out_dtype=F32, tm=1024, tn=1024,
                     tk=4096, resid=(x2, mod6, 5), rows_per_batch=s)
    return x2.reshape(b, s, d)
```

```python
import functools

import jax
import jax.numpy as jnp
from jax import lax
from jax.experimental import pallas as pl
from jax.experimental.pallas import tpu as pltpu

HEAD_DIM = 128
CHUNK = 64
SUB = 16
N_SUB = CHUNK // SUB
N_MOD = 6
EPS = 1e-6
MASK_VALUE = -1e30
K_MAX = 1.0 - 1e-6
LOG2E = 1.4426950408889634
VMEM_LIMIT_BYTES = 60 * 1024 * 1024

F32 = jnp.float32
BF16 = jnp.bfloat16

_NT = (((1,), (1,)), ((), ()))
_TN = (((0,), (0,)), ((), ()))


def _params(*sem):
    return pltpu.CompilerParams(dimension_semantics=sem, vmem_limit_bytes=VMEM_LIMIT_BYTES)


def _fit_tile(tile, dim):
    tile = min(tile, dim)
    while dim % tile:
        tile -= HEAD_DIM
    return tile


def _sigmoid(x):
    return 1.0 / (1.0 + jnp.exp(-x))


def _tril_cumsum(tril, x):
    hi = x.astype(BF16)
    r1 = x - hi.astype(F32)
    mid = r1.astype(BF16)
    lo = (r1 - mid.astype(F32)).astype(BF16)
    out = jnp.dot(tril, hi, preferred_element_type=F32)
    out = out + jnp.dot(tril, mid, preferred_element_type=F32)
    return out + jnp.dot(tril, lo, preferred_element_type=F32)


def _tril(n):
    r = lax.broadcasted_iota(jnp.int32, (n, n), 0)
    c = lax.broadcasted_iota(jnp.int32, (n, n), 1)
    return (r >= c).astype(BF16)


def _mod_kernel(c_ref, w_ref, b_ref, o_ref):
    c = c_ref[...]
    ca = (c * _sigmoid(c)).astype(BF16)
    w = w_ref[...].astype(BF16)
    o_ref[...] = jnp.dot(ca, w, preferred_element_type=F32) + b_ref[...]


def _adaln_mod(c_pad, w_ada, b_ada):
    depth, d, n = w_ada.shape
    rows = c_pad.shape[0]
    tn = min(512, n)
    return pl.pallas_call(
        _mod_kernel,
        grid=(depth, n // tn),
        in_specs=[
            pl.BlockSpec((rows, d), lambda l, j: (0, 0)),
            pl.BlockSpec((None, d, tn), lambda l, j: (l, 0, j)),
            pl.BlockSpec((None, 1, tn), lambda l, j: (l, 0, j)),
        ],
        out_specs=pl.BlockSpec((None, rows, tn), lambda l, j: (l, 0, j)),
        out_shape=jax.ShapeDtypeStruct((depth, rows, n), F32),
        compiler_params=_params("arbitrary", "arbitrary"),
        name="adaln_mod",
    )(c_pad, w_ada, b_ada.reshape(depth, 1, n))


def _norm_mod_kernel(x_ref, w_ref, scale_ref, shift_ref, *rest, n_valid):
    x = x_ref[...]
    ms = jnp.mean(x * x, axis=-1, keepdims=True)
    y = x * lax.rsqrt(ms + EPS) * w_ref[...]
    h = (y * (1.0 + scale_ref[...]) + shift_ref[...]).astype(BF16)
    if n_valid is None:
        o_ref, = rest
    else:
        wfg_ref, o_ref, fg_ref = rest
        row = lax.broadcasted_iota(jnp.int32, wfg_ref.shape, 0)
        wfg = jnp.where(row < n_valid, wfg_ref[...], 0.0).astype(BF16)
        fg_ref[...] = lax.dot_general(h, wfg, _NT, preferred_element_type=F32)
    o_ref[...] = h


def _norm_mod(x3, w, mod6, scale_idx, shift_idx, fg_weights=None):
    b, s, d = x3.shape
    ts = min(512, s)
    in_specs = [
        pl.BlockSpec((None, ts, d), lambda i, t: (i, t, 0)),
        pl.BlockSpec((1, d), lambda i, t: (0, 0)),
        pl.BlockSpec((None, 1, d), lambda i, t: (i * N_MOD + scale_idx, 0, 0)),
        pl.BlockSpec((None, 1, d), lambda i, t: (i * N_MOD + shift_idx, 0, 0)),
    ]
    args = [x3, w.reshape(1, d), mod6, mod6]
    out_specs = [pl.BlockSpec((None, ts, d), lambda i, t: (i, t, 0))]
    out_shape = [jax.ShapeDtypeStruct((b, s, d), BF16)]
    n_valid = None
    if fg_weights is not None:
        w_in_t, layer, row0 = fg_weights
        n_valid = w_in_t.shape[1] - row0
        assert row0 % HEAD_DIM == 0 and 0 < n_valid <= HEAD_DIM and w_in_t.shape[2] == d
        in_specs.append(pl.BlockSpec((None, HEAD_DIM, d),
                                     lambda i, t: (layer, row0 // HEAD_DIM, 0)))
        args.append(w_in_t)
        out_specs.append(pl.BlockSpec((None, ts, HEAD_DIM), lambda i, t: (i, t, 0)))
        out_shape.append(jax.ShapeDtypeStruct((b, s, HEAD_DIM), F32))
    outs = pl.pallas_call(
        functools.partial(_norm_mod_kernel, n_valid=n_valid),
        grid=(b, s // ts),
        in_specs=in_specs,
        out_specs=out_specs,
        out_shape=out_shape,
        compiler_params=_params("arbitrary", "arbitrary"),
        name="norm_mod",
    )(*args)
    return outs if fg_weights is not None else outs[0]


def _slab_dots(a_refs, w_ref, w_t=False):
    part = None
    off = 0
    for a_ref in a_refs:
        ka = a_ref.shape[1]
        if w_t:
            p = lax.dot_general(a_ref[...], w_ref[:, off:off + ka], _NT,
                                preferred_element_type=F32)
        else:
            p = jnp.dot(a_ref[...], w_ref[off:off + ka, :], preferred_element_type=F32)
        part = p if part is None else part + p
        off += ka
    return part


def _mm_epilogue(mode, acc, o_ref, x_ref, g_ref):
    if mode == "plain":
        o_ref[...] = acc.astype(o_ref.dtype)
    elif mode == "relu2":
        r = jnp.maximum(acc, 0.0)
        o_ref[...] = (r * r).astype(o_ref.dtype)
    else:
        o_ref[...] = x_ref[...] + g_ref[...] * acc


W_CAST_ROWS = 512


def _mm_wcast_kernel(*refs, n_a, mode, layer, n_col_tiles, tn, w_t, side):
    a_refs = refs[:n_a]
    w_hbm = refs[n_a]
    rest = list(refs[n_a + 1:])
    x_ref, g_ref = (rest.pop(0), rest.pop(0)) if mode == "resid" else (None, None)
    side_in = rest.pop(0) if side else None
    o_ref = rest.pop(0)
    side_out = rest.pop(0) if side else None
    wf_s, wb_s, sem = rest
    j = pl.program_id(0)

    if side:
        side_out[...] = side_in[...].astype(side_out.dtype)

    def tile_copy(jj):
        sel = pl.ds(pl.multiple_of(jj * tn, tn), tn)
        src = w_hbm.at[layer, sel, :] if w_t else w_hbm.at[layer, :, sel]
        return pltpu.make_async_copy(src, wf_s, sem)

    @pl.when(pl.program_id(1) == 0)
    def _():
        @pl.when(j == 0)
        def _():
            tile_copy(0).start()

        tile_copy(j).wait()
        n_rows = wf_s.shape[0]
        for r0 in range(0, n_rows, W_CAST_ROWS):
            r1 = min(r0 + W_CAST_ROWS, n_rows)
            wb_s[r0:r1, :] = wf_s[r0:r1, :].astype(BF16)

        @pl.when(j + 1 < n_col_tiles)
        def _():
            tile_copy(j + 1).start()

    _mm_epilogue(mode, _slab_dots(a_refs, wb_s, w_t), o_ref, x_ref, g_ref)


def _matmul_wcast(a_list, w_stack, layer, *, mode, out_dtype, tm, tn, n_cols=None, w_t=False,
                  resid=None, rows_per_batch=None, side_cast=None):
    m = a_list[0].shape[0]
    if w_t:
        _, n_total, kdim = w_stack.shape
    else:
        _, kdim, n_total = w_stack.shape
    n = n_total if n_cols is None else n_cols
    n_a = len(a_list)
    tm = _fit_tile(tm, m)
    tn = _fit_tile(tn, n)
    w_tile = (tn, kdim) if w_t else (kdim, tn)
    assert m % tm == 0 and n % tn == 0
    assert sum(a.shape[1] for a in a_list) == kdim

    in_specs = [pl.BlockSpec((tm, a.shape[1]), lambda j, i: (i, 0)) for a in a_list]
    in_specs.append(pl.BlockSpec(memory_space=pl.ANY))
    args = list(a_list) + [w_stack]
    if mode == "resid":
        x2, mod6, gate_idx = resid
        assert rows_per_batch % tm == 0
        tiles_per_batch = rows_per_batch // tm
        in_specs.append(pl.BlockSpec((tm, tn), lambda j, i: (i, j)))
        in_specs.append(pl.BlockSpec(
            (None, 1, tn), lambda j, i: ((i // tiles_per_batch) * N_MOD + gate_idx, 0, j)))
        args += [x2, mod6]
    out_specs = [pl.BlockSpec((tm, tn), lambda j, i: (i, j))]
    out_shape = [jax.ShapeDtypeStruct((m, n), out_dtype)]
    n_i = m // tm
    if side_cast is not None:
        side_stack, side_layer = side_cast
        _, s_rows, s_cols = side_stack.shape
        n_steps = (n // tn) * n_i
        rs = s_rows // n_steps
        assert rs * n_steps == s_rows and rs % 16 == 0
        in_specs.append(pl.BlockSpec((None, rs, s_cols), lambda j, i: (side_layer, j * n_i + i, 0)))
        args.append(side_stack)
        out_specs.append(pl.BlockSpec((rs, s_cols), lambda j, i: (j * n_i + i, 0)))
        out_shape.append(jax.ShapeDtypeStruct((s_rows, s_cols), BF16))
    outs = pl.pallas_call(
        functools.partial(_mm_wcast_kernel, n_a=n_a, mode=mode, layer=layer,
                          n_col_tiles=n // tn, tn=tn, w_t=w_t, side=side_cast is not None),
        grid=(n // tn, n_i),
        in_specs=in_specs,
        out_specs=out_specs,
        out_shape=out_shape,
        scratch_shapes=[pltpu.VMEM(w_tile, F32), pltpu.VMEM(w_tile, BF16),
                        pltpu.SemaphoreType.DMA(())],
        compiler_params=_params("arbitrary", "arbitrary"),
        name="mmw_" + mode,
    )(*args)
    return outs if side_cast is not None else outs[0]


def _mm_kernel(*refs, n_a, nk, mode):
    a_refs = refs[:n_a]
    w_ref = refs[n_a]
    if mode == "resid":
        x_ref, g_ref, o_ref = refs[n_a + 1:]
    else:
        o_ref, = refs[n_a + 1:]
        x_ref = g_ref = None

    part = _slab_dots(a_refs, w_ref)

    def finish(acc):
        _mm_epilogue(mode, acc, o_ref, x_ref, g_ref)

    if nk == 1:
        finish(part)
    else:
        k = pl.program_id(2)

        @pl.when(k == 0)
        def _():
            o_ref[...] = part

        @pl.when(jnp.logical_and(k > 0, k < nk - 1))
        def _():
            o_ref[...] += part

        @pl.when(k == nk - 1)
        def _():
            finish(o_ref[...] + part)


def _matmul(a_list, w_stack, layer, *, mode, out_dtype, tm, tn, tk=None, n_cols=None,
            resid=None, rows_per_batch=None):
    m = a_list[0].shape[0]
    _, kdim, n_total = w_stack.shape
    n = n_total if n_cols is None else n_cols
    n_a = len(a_list)
    tm = _fit_tile(tm, m)
    tn = _fit_tile(tn, n)
    if tk is None or tk >= kdim:
        tk = kdim
    nk = kdim // tk
    assert n_a == 1 or nk == 1
    assert nk == 1 or out_dtype == F32
    assert m % tm == 0 and n % tn == 0 and kdim % tk == 0

    in_specs = []
    if n_a == 1:
        in_specs.append(pl.BlockSpec((tm, tk), lambda i, j, k: (i, k)))
    else:
        for a in a_list:
            in_specs.append(pl.BlockSpec((tm, a.shape[1]), lambda i, j, k: (i, 0)))
    in_specs.append(pl.BlockSpec((None, tk, tn), lambda i, j, k: (layer, k, j)))
    args = list(a_list) + [w_stack]
    if mode == "resid":
        x2, mod6, gate_idx = resid
        assert rows_per_batch % tm == 0
        tiles_per_batch = rows_per_batch // tm
        in_specs.append(pl.BlockSpec((tm, tn), lambda i, j, k: (i, j)))
        in_specs.append(pl.BlockSpec(
            (None, 1, tn), lambda i, j, k: ((i // tiles_per_batch) * N_MOD + gate_idx, 0, j)))
        args += [x2, mod6]
    return pl.pallas_call(
        functools.partial(_mm_kernel, n_a=n_a, nk=nk, mode=mode),
        grid=(m // tm, n // tn, nk),
        in_specs=in_specs,
        out_specs=pl.BlockSpec((tm, tn), lambda i, j, k: (i, j)),
        out_shape=jax.ShapeDtypeStruct((m, n), out_dtype),
        compiler_params=_params("arbitrary", "arbitrary", "arbitrary"),
        name="mm_" + mode,
    )(*args)


def _fcum_kernel(fg_ref, bias_ref, fc_ref):
    n_slab = fg_ref.shape[0] // HEAD_DIM
    tril = _tril(HEAD_DIM)

    def body(r, carry):
        rows = pl.ds(pl.multiple_of(r * HEAD_DIM, HEAD_DIM), HEAD_DIM)
        z = fg_ref[rows, :] + bias_ref[...]
        lf = jnp.minimum(z, 0.0) - jnp.log1p(jnp.exp(-jnp.abs(z)))
        f = _tril_cumsum(tril, lf) + carry
        fc_ref[rows, :] = f
        return f[HEAD_DIM - 1:HEAD_DIM, :]

    lax.fori_loop(0, n_slab, body, jnp.zeros((1, HEAD_DIM), F32))


def _fcum(fg3, bias_row):
    b, s, _ = fg3.shape
    return pl.pallas_call(
        _fcum_kernel,
        grid=(b,),
        in_specs=[
            pl.BlockSpec((None, s, HEAD_DIM), lambda i: (i, 0, 0)),
            pl.BlockSpec((1, HEAD_DIM), lambda i: (0, 0)),
        ],
        out_specs=pl.BlockSpec((None, s, HEAD_DIM), lambda i: (i, 0, 0)),
        out_shape=jax.ShapeDtypeStruct((b, s, HEAD_DIM), F32),
        compiler_params=_params("arbitrary"),
        name="fox_fcum",
    )(fg3, bias_row)


V_ROWS = HEAD_DIM + 16


def _fox_kernel(q_ref, k_ref, v_ref, fc_ref, qn_ref, kn_ref, o_ref,
                kaug_s, vaug_s, qaug_s, s0_s, s1_s, m_s, acc_s, *, blk, n_blk):
    h = pl.program_id(1)
    qi = pl.program_id(2)
    qblk = 2 * blk
    lane = lax.broadcasted_iota(jnp.int32, (blk, HEAD_DIM), 1)

    @pl.when(qi == 0)
    def _():
        ones_row = (lax.broadcasted_iota(jnp.int32, (V_ROWS - HEAD_DIM, blk), 0) == 0)
        for c in range(n_blk):
            rows = slice(c * blk, (c + 1) * blk)
            k = k_ref[rows, :]
            ms = jnp.mean(k * k, axis=-1, keepdims=True)
            kaug_s[rows, 0:HEAD_DIM] = (k * lax.rsqrt(ms + EPS) * kn_ref[...]).astype(BF16)
            fcol = jnp.sum(jnp.where(lane == h, fc_ref[rows, :], 0.0), axis=1, keepdims=True)
            nf = fcol * (-LOG2E)
            hi = nf.astype(BF16).astype(F32)
            mid = (nf - hi).astype(BF16).astype(F32)
            lo = (nf - hi) - mid
            aug = jnp.where(lane == 0, hi, jnp.where(lane == 1, mid, jnp.where(lane == 2, lo, 0.0)))
            kaug_s[rows, HEAD_DIM:2 * HEAD_DIM] = aug.astype(BF16)
            vaug_s[c, 0:HEAD_DIM, :] = v_ref[rows, :].T.astype(BF16)
            vaug_s[c, HEAD_DIM:V_ROWS, :] = ones_row.astype(BF16)

    q = q_ref[...]
    ms = jnp.mean(q * q, axis=-1, keepdims=True)
    qaug_s[:, 0:HEAD_DIM] = (q * lax.rsqrt(ms + EPS) * qn_ref[...]
                             * (HEAD_DIM ** -0.5 * LOG2E)).astype(BF16)
    qaug_s[:, HEAD_DIM:2 * HEAD_DIM] = (
        lax.broadcasted_iota(jnp.int32, (qblk, HEAD_DIM), 1) < 3).astype(BF16)
    m_s[...] = jnp.full(m_s.shape, MASK_VALUE, F32)
    acc_s[...] = jnp.zeros(acc_s.shape, F32)

    def scores(kj, q0=0):
        rows = pl.ds(pl.multiple_of(kj * blk, blk), blk)
        return lax.dot_general(kaug_s[rows, :], qaug_s[q0:, :], _NT, preferred_element_type=F32)

    def update(s_ref, kj, causal, q0=0):
        st = s_ref[:, q0:]
        if causal:
            kr = lax.broadcasted_iota(jnp.int32, st.shape, 0)
            qc = lax.broadcasted_iota(jnp.int32, st.shape, 1)
            st = jnp.where(kr <= qc, st, MASK_VALUE)
        m_old = m_s[:, q0:]
        m_new = jnp.maximum(m_old, jnp.max(st, axis=0, keepdims=True))
        pt = jnp.exp2(st - m_new).astype(BF16)
        alpha = jnp.exp2(m_old - m_new)
        acc_s[:, q0:] = alpha * acc_s[:, q0:] + jnp.dot(vaug_s[kj], pt,
                                                        preferred_element_type=F32)
        m_s[:, q0:] = m_new

    s0_s[...] = scores(0)

    def pair(i, carry):
        kj = 2 * i
        s1_s[...] = scores(kj + 1)
        update(s0_s, kj, False)
        s0_s[...] = scores(kj + 2)
        update(s1_s, kj + 1, False)
        return carry

    lax.fori_loop(0, qi, pair, 0)
    s1_s[:, blk:] = scores(2 * qi + 1, blk)
    update(s0_s, 2 * qi, True)
    update(s1_s, 2 * qi + 1, True, blk)

    acc = acc_s[...]
    ot = acc[0:HEAD_DIM, :] / acc[HEAD_DIM:HEAD_DIM + 1, :]
    o_ref[...] = ot.T.astype(o_ref.dtype)


def _fox(proj3, fc, qn_w, kn_w, *, n_heads, q_blk0, k_blk0, v_blk0, blk):
    b, s, _ = proj3.shape
    n_blk = s // blk
    qblk = 2 * blk
    assert s % qblk == 0
    return pl.pallas_call(
        functools.partial(_fox_kernel, blk=blk, n_blk=n_blk),
        grid=(b, n_heads, s // qblk),
        in_specs=[
            pl.BlockSpec((None, qblk, HEAD_DIM), lambda i, h, t: (i, t, q_blk0 + h)),
            pl.BlockSpec((None, s, HEAD_DIM), lambda i, h, t: (i, 0, k_blk0 + h)),
            pl.BlockSpec((None, s, HEAD_DIM), lambda i, h, t: (i, 0, v_blk0 + h)),
            pl.BlockSpec((None, s, HEAD_DIM), lambda i, h, t: (i, 0, 0)),
            pl.BlockSpec((1, HEAD_DIM), lambda i, h, t: (0, 0)),
            pl.BlockSpec((1, HEAD_DIM), lambda i, h, t: (0, 0)),
        ],
        out_specs=pl.BlockSpec((None, qblk, HEAD_DIM), lambda i, h, t: (i, t, h)),
        out_shape=jax.ShapeDtypeStruct((b, s, n_heads * HEAD_DIM), BF16),
        scratch_shapes=[
            pltpu.VMEM((s, 2 * HEAD_DIM), BF16),
            pltpu.VMEM((n_blk, V_ROWS, blk), BF16),
            pltpu.VMEM((qblk, 2 * HEAD_DIM), BF16),
            pltpu.VMEM((blk, qblk), F32),
            pltpu.VMEM((blk, qblk), F32),
            pltpu.VMEM((1, qblk), F32),
            pltpu.VMEM((V_ROWS, qblk), F32),
        ],
        compiler_params=_params("arbitrary", "arbitrary", "arbitrary"),
        name="fox_attn",
    )(proj3, proj3, proj3, fc, qn_w.reshape(1, HEAD_DIM), kn_w.reshape(1, HEAD_DIM))


def _hgrn2_kernel(q_ref, f_ref, i_ref, g_ref, lb_ref, nw_ref, seg_ref, o_ref,
                  st_s, y_s, g_s, kk_s, qf_s, r_s, a_s, qe_s, u_s, dec_s, *, layer, rows):
    t_id = pl.program_id(2)
    n_chunk = rows // CHUNK

    @pl.when(t_id == 0)
    def _():
        st_s[...] = jnp.zeros(st_s.shape, F32)

    depth = lb_ref.shape[0]
    lbs = [lb_ref[j] for j in range(depth)]
    mx = lbs[0]
    for j in range(1, depth):
        mx = jnp.maximum(mx, lbs[j])
    es = [jnp.exp(v - mx) for v in lbs]
    den = es[0]
    for j in range(1, depth):
        den = den + es[j]
    lb = jnp.zeros((1, HEAD_DIM), F32)
    for j in range(1, layer + 1):
        lb = lb + es[j] / den

    tril = _tril(CHUNK)

    half = SUB // 2
    zeros_half = jnp.zeros((half, HEAD_DIM), F32)
    row_half = lax.broadcasted_iota(jnp.int32, (half, HEAD_DIM), 0)

    def prep(c, carry):
        rws = pl.ds(pl.multiple_of(c * CHUNK, CHUNK), CHUNK)
        z = f_ref[rws, :]
        kk = (1.0 - lb) * _sigmoid(-z)
        logf = jnp.log1p(-jnp.minimum(kk, K_MAX))
        gc = _tril_cumsum(tril, logf) * LOG2E
        q = q_ref[rws, :]
        qf = q * _sigmoid(q)
        kk_s[rws, :] = kk
        g_s[rws, :] = gc
        qf_s[rws, :] = qf
        for b in range(N_SUB):
            r0 = b * SUB
            q_t, q_b = qf[r0:r0 + half], qf[r0 + half:r0 + SUB]
            k_t, k_b = kk[r0:r0 + half], kk[r0 + half:r0 + SUB]
            g_t, g_b = gc[r0:r0 + half], gc[r0 + half:r0 + SUB]
            m_t = g_t[half // 2:half // 2 + 1]
            m_b = g_b[half // 2:half // 2 + 1]
            g_8 = g_b[0:1]
            qt_f = q_t * jnp.exp2(g_t - m_t)
            pt_f = k_t * jnp.exp2(m_t - g_t)
            qb_f = q_b * jnp.exp2(g_b - m_b)
            pb_f = k_b * jnp.exp2(m_b - g_b)
            qx_f = q_b * jnp.exp2(g_b - g_8)
            px_f = k_t * jnp.exp2(g_8 - g_t)
            yrows = pl.ds(pl.multiple_of(c * CHUNK + r0, SUB), SUB)
            for s in range(SUB):
                if s < half:
                    top = jnp.where(row_half >= s, qt_f * pt_f[s:s + 1], 0.0)
                    bot = qx_f * px_f[s:s + 1]
                else:
                    top = zeros_half
                    bot = jnp.where(row_half >= s - half, qb_f * pb_f[s - half:s - half + 1], 0.0)
                x = jnp.concatenate([top, bot], axis=0)
                y_s[yrows, s * HEAD_DIM:(s + 1) * HEAD_DIM] = x.astype(BF16)
        return carry

    lax.fori_loop(0, n_chunk, prep, 0, unroll=True)

    r_s[...] = jnp.dot(y_s[...], seg_ref[...], preferred_element_type=F32)

    tr = lax.broadcasted_iota(jnp.int32, (CHUNK, CHUNK), 0)
    tc = lax.broadcasted_iota(jnp.int32, (CHUNK, CHUNK), 1)
    same_blk = (tr // SUB) == (tc // SUB)
    diag_mask = jnp.logical_and(same_blk, tc <= tr)

    def scores(c, carry):
        rws = pl.ds(pl.multiple_of(c * CHUNK, CHUNK), CHUNK)
        qf = qf_s[rws, :]
        kk = kk_s[rws, :]
        gc = g_s[rws, :]
        blocks = [jnp.zeros((SUB, CHUNK), F32)]
        for i in range(1, N_SUB):
            lo = i * SUB
            gn = gc[lo:lo + 1]
            ql = qf[lo:lo + SUB] * jnp.exp2(gc[lo:lo + SUB] - gn)
            kr = jnp.concatenate([kk[:lo] * jnp.exp2(gn - gc[:lo]),
                                  jnp.zeros((CHUNK - lo, HEAD_DIM), F32)], axis=0)
            blocks.append(lax.dot_general(ql.astype(BF16), kr.astype(BF16), _NT,
                                          preferred_element_type=F32))
        a_off = jnp.concatenate(blocks, axis=0)
        a = jnp.where(diag_mask, r_s[rws, :][:, :CHUNK], a_off)
        a_s[rws, :] = a.astype(BF16)
        g_last = gc[CHUNK - 1:CHUNK]
        qe_s[rws, :] = (qf * jnp.exp2(gc)).astype(BF16)
        kd = kk * jnp.exp2(g_last - gc)
        u_s[c] = lax.dot_general(i_ref[rws, :].astype(BF16), kd.astype(BF16), _TN,
                                 preferred_element_type=F32)
        dec_s[c] = jnp.exp2(g_last)
        return carry

    lax.fori_loop(0, n_chunk, scores, 0, unroll=True)

    def chunk(c, carry):
        rws = pl.ds(pl.multiple_of(c * CHUNK, CHUNK), CHUNK)
        vb = i_ref[rws, :].astype(BF16)
        st = st_s[...]
        o = jnp.dot(a_s[rws, :], vb, preferred_element_type=F32)
        o = o + lax.dot_general(qe_s[rws, :], st.astype(BF16), _NT,
                                preferred_element_type=F32)
        st_s[...] = dec_s[c] * st + u_s[c]
        ms = jnp.mean(o * o, axis=-1, keepdims=True)
        y = o * lax.rsqrt(ms + EPS) * nw_ref[...]
        g = g_ref[rws, :]
        o_ref[rws, :] = (y * (g * _sigmoid(g))).astype(o_ref.dtype)
        return carry

    lax.fori_loop(0, n_chunk, chunk, 0, unroll=True)


def _hgrn2(proj3, lower_bounds, nw, *, layer, n_heads, rows):
    b, s, _ = proj3.shape
    depth = lower_bounds.shape[0]
    rows = min(rows, s)
    seg_r = jnp.arange(SUB * HEAD_DIM, dtype=jnp.int32)[:, None] // HEAD_DIM
    seg_c = jnp.arange(HEAD_DIM, dtype=jnp.int32)[None, :]
    seg = jnp.logical_and(seg_c < CHUNK, (seg_c % SUB) == seg_r).astype(BF16)
    lb4 = lower_bounds.astype(F32).reshape(depth, n_heads, 1, HEAD_DIM)

    def col(off):
        return pl.BlockSpec((None, rows, HEAD_DIM), lambda i, h, t: (i, t, off * n_heads + h))

    return pl.pallas_call(
        functools.partial(_hgrn2_kernel, layer=layer, rows=rows),
        grid=(b, n_heads, s // rows),
        in_specs=[
            col(0), col(1), col(2), col(3),
            pl.BlockSpec((depth, None, 1, HEAD_DIM), lambda i, h, t: (0, h, 0, 0)),
            pl.BlockSpec((1, HEAD_DIM), lambda i, h, t: (0, 0)),
            pl.BlockSpec((SUB * HEAD_DIM, HEAD_DIM), lambda i, h, t: (0, 0)),
        ],
        out_specs=pl.BlockSpec((None, rows, HEAD_DIM), lambda i, h, t: (i, t, h)),
        out_shape=jax.ShapeDtypeStruct((b, s, n_heads * HEAD_DIM), BF16),
        scratch_shapes=[
            pltpu.VMEM((HEAD_DIM, HEAD_DIM), F32),
            pltpu.VMEM((rows, SUB * HEAD_DIM), BF16),
            pltpu.VMEM((rows, HEAD_DIM), F32),
            pltpu.VMEM((rows, HEAD_DIM), F32),
            pltpu.VMEM((rows, HEAD_DIM), F32),
            pltpu.VMEM((rows, HEAD_DIM), F32),
            pltpu.VMEM((rows, CHUNK), BF16),
            pltpu.VMEM((rows, HEAD_DIM), BF16),
            pltpu.VMEM((rows // CHUNK, HEAD_DIM, HEAD_DIM), F32),
            pltpu.VMEM((rows // CHUNK, 1, HEAD_DIM), F32),
        ],
        compiler_params=_params("arbitrary", "arbitrary", "arbitrary"),
        name="hgrn2",
    )(proj3, proj3, proj3, proj3, lb4, nw.reshape(1, HEAD_DIM), seg)


def kernel(x, c, lower_bounds, w_ada, b_ada, norm_mix_w, norm_ffn_w, w_in, rec_norm_w,
           fg_bias, q_norm_w, k_norm_w, w_out, w_up, w_down):
    b, s, d = x.shape
    depth = w_ada.shape[0]
    n_heads = d // HEAD_DIM
    h_rec = n_heads // 2
    h_att = n_heads - h_rec
    main_cols = (4 * h_rec + 3 * h_att) * HEAD_DIM
    assert w_in.shape[2] == main_cols + h_att and h_att <= HEAD_DIM
    assert s % CHUNK == 0 and s % HEAD_DIM == 0
    m = b * s
    fox_blk = min(512, s // 2)

    rows_pad = -(-b // 8) * 8
    c_pad = jnp.zeros((rows_pad, d), F32).at[:b].set(c.astype(F32))
    mod = _adaln_mod(c_pad, w_ada, b_ada)

    w_in_t = jnp.swapaxes(w_in, 1, 2)

    x2 = x.reshape(m, d)
    for l in range(depth):
        mod6 = mod[l, :b].reshape(b * N_MOD, 1, d)

        h, fg3 = _norm_mod(x2.reshape(b, s, d), norm_mix_w[l], mod6, 1, 0,
                           fg_weights=(w_in_t, l, main_cols))
        proj = _matmul_wcast([h.reshape(m, d)], w_in_t, l, mode="plain", out_dtype=F32,
                             tm=1024, tn=1024, n_cols=main_cols, w_t=True)
        proj3 = proj.reshape(b, s, main_cols)

        o_rec = _hgrn2(proj3, lower_bounds, rec_norm_w[l], layer=l, n_heads=h_rec, rows=2048)

        bias_row = jnp.pad(fg_bias[l].astype(F32), (0, HEAD_DIM - h_att)).reshape(1, HEAD_DIM)
        fc = _fcum(fg3, bias_row)
        o_att = _fox(proj3, fc, q_norm_w[l], k_norm_w[l], n_heads=h_att,
                     q_blk0=4 * h_rec, k_blk0=4 * h_rec + h_att, v_blk0=4 * h_rec + 2 * h_att,
                     blk=fox_blk)

        x2 = _matmul_wcast([o_rec.reshape(m, -1), o_att.reshape(m, -1)], w_out, l, mode="resid",
                           out_dtype=F32, tm=1024, tn=512, resid=(x2, mod6, 2),
                           rows_per_batch=s)

        h2 = _norm_mod(x2.reshape(b, s, d), norm_ffn_w[l], mod6, 4, 3).reshape(m, d)
        u, w_down_b = _matmul_wcast([h2], w_up, l, mode="relu2", out_dtype=BF16, tm=1024,
                                    tn=1024, side_cast=(w_down, l))
        x2 = _matmul([u], w_down_b[None], 0, mode="resid", out_dtype=F32, tm=1024, tn=1024,
                     tk=4096, resid=(x2, mod6, 5), rows_per_batch=s)
    return x2.reshape(b, s, d)
```

```python
import functools

import jax
import jax.numpy as jnp
from jax import lax
from jax.experimental import pallas as pl
from jax.experimental.pallas import tpu as pltpu

HEAD_DIM = 128
CHUNK = 64
SUB = 16
N_SUB = CHUNK // SUB
N_MOD = 6
EPS = 1e-6
MASK_VALUE = -1e30
K_MAX = 1.0 - 1e-6
LOG2E = 1.4426950408889634
VMEM_LIMIT_BYTES = 60 * 1024 * 1024

MM_TM = 1024
MM_TN = 1024
W_OUT_TN = 512
W_DOWN_TK = 4096
ADALN_TN = 512
NORM_ROWS = 512
FOX_KEY_BLK = 512
HGRN2_ROWS = 2048

F32 = jnp.float32
BF16 = jnp.bfloat16

_NT = (((1,), (1,)), ((), ()))
_TN = (((0,), (0,)), ((), ()))


def _params(*sem):
    return pltpu.CompilerParams(dimension_semantics=sem, vmem_limit_bytes=VMEM_LIMIT_BYTES)


def _fit_tile(tile, dim):
    tile = min(tile, dim)
    while dim % tile:
        tile -= HEAD_DIM
    return tile


def _sigmoid(x):
    return 1.0 / (1.0 + jnp.exp(-x))


def _tril_cumsum(tril, x):
    hi = x.astype(BF16)
    r1 = x - hi.astype(F32)
    mid = r1.astype(BF16)
    lo = (r1 - mid.astype(F32)).astype(BF16)
    out = jnp.dot(tril, hi, preferred_element_type=F32)
    out = out + jnp.dot(tril, mid, preferred_element_type=F32)
    return out + jnp.dot(tril, lo, preferred_element_type=F32)


def _tril(n):
    r = lax.broadcasted_iota(jnp.int32, (n, n), 0)
    c = lax.broadcasted_iota(jnp.int32, (n, n), 1)
    return (r >= c).astype(BF16)


def _mod_kernel(c_ref, w_ref, b_ref, o_ref):
    c = c_ref[...]
    ca = (c * _sigmoid(c)).astype(BF16)
    w = w_ref[...].astype(BF16)
    o_ref[...] = jnp.dot(ca, w, preferred_element_type=F32) + b_ref[...]


def _adaln_mod(c_pad, w_ada, b_ada):
    depth, d, n = w_ada.shape
    rows = c_pad.shape[0]
    tn = _fit_tile(ADALN_TN, n)
    return pl.pallas_call(
        _mod_kernel,
        grid=(depth, n // tn),
        in_specs=[
            pl.BlockSpec((rows, d), lambda l, j: (0, 0)),
            pl.BlockSpec((None, d, tn), lambda l, j: (l, 0, j)),
            pl.BlockSpec((None, 1, tn), lambda l, j: (l, 0, j)),
        ],
        out_specs=pl.BlockSpec((None, rows, tn), lambda l, j: (l, 0, j)),
        out_shape=jax.ShapeDtypeStruct((depth, rows, n), F32),
        compiler_params=_params("arbitrary", "arbitrary"),
        name="adaln_mod",
    )(c_pad, w_ada, b_ada.reshape(depth, 1, n))


def _norm_mod_kernel(x_ref, w_ref, scale_ref, shift_ref, *rest, n_valid):
    x = x_ref[...]
    ms = jnp.mean(x * x, axis=-1, keepdims=True)
    y = x * lax.rsqrt(ms + EPS) * w_ref[...]
    h = (y * (1.0 + scale_ref[...]) + shift_ref[...]).astype(BF16)
    if n_valid is None:
        o_ref, = rest
    else:
        wfg_ref, o_ref, fg_ref = rest
        row = lax.broadcasted_iota(jnp.int32, wfg_ref.shape, 0)
        wfg = jnp.where(row < n_valid, wfg_ref[...], 0.0).astype(BF16)
        fg_ref[...] = lax.dot_general(h, wfg, _NT, preferred_element_type=F32)
    o_ref[...] = h


def _norm_mod(x3, w, mod6, scale_idx, shift_idx, fg_weights=None):
    b, s, d = x3.shape
    ts = min(NORM_ROWS, s)
    in_specs = [
        pl.BlockSpec((None, ts, d), lambda i, t: (i, t, 0)),
        pl.BlockSpec((1, d), lambda i, t: (0, 0)),
        pl.BlockSpec((None, 1, d), lambda i, t: (i * N_MOD + scale_idx, 0, 0)),
        pl.BlockSpec((None, 1, d), lambda i, t: (i * N_MOD + shift_idx, 0, 0)),
    ]
    args = [x3, w.reshape(1, d), mod6, mod6]
    out_specs = [pl.BlockSpec((None, ts, d), lambda i, t: (i, t, 0))]
    out_shape = [jax.ShapeDtypeStruct((b, s, d), BF16)]
    n_valid = None
    if fg_weights is not None:
        w_in_t, layer, row0 = fg_weights
        n_valid = w_in_t.shape[1] - row0
        assert row0 % HEAD_DIM == 0 and 0 < n_valid <= HEAD_DIM and w_in_t.shape[2] == d
        in_specs.append(pl.BlockSpec((None, HEAD_DIM, d),
                                     lambda i, t: (layer, row0 // HEAD_DIM, 0)))
        args.append(w_in_t)
        out_specs.append(pl.BlockSpec((None, ts, HEAD_DIM), lambda i, t: (i, t, 0)))
        out_shape.append(jax.ShapeDtypeStruct((b, s, HEAD_DIM), F32))
    outs = pl.pallas_call(
        functools.partial(_norm_mod_kernel, n_valid=n_valid),
        grid=(b, s // ts),
        in_specs=in_specs,
        out_specs=out_specs,
        out_shape=out_shape,
        compiler_params=_params("arbitrary", "arbitrary"),
        name="norm_mod",
    )(*args)
    return outs if fg_weights is not None else outs[0]


def _slab_dots(a_refs, w_ref, w_t=False):
    part = None
    off = 0
    for a_ref in a_refs:
        ka = a_ref.shape[1]
        if w_t:
            p = lax.dot_general(a_ref[...], w_ref[:, off:off + ka], _NT,
                                preferred_element_type=F32)
        else:
            p = jnp.dot(a_ref[...], w_ref[off:off + ka, :], preferred_element_type=F32)
        part = p if part is None else part + p
        off += ka
    return part


def _mm_epilogue(mode, acc, o_ref, x_ref, g_ref):
    if mode == "plain":
        o_ref[...] = acc.astype(o_ref.dtype)
    elif mode == "relu2":
        r = jnp.maximum(acc, 0.0)
        o_ref[...] = (r * r).astype(o_ref.dtype)
    else:
        o_ref[...] = x_ref[...] + g_ref[...] * acc


W_CAST_ROWS = 512


def _mm_wcast_kernel(*refs, n_a, mode, layer, n_col_tiles, tn, w_t, side):
    a_refs = refs[:n_a]
    w_hbm = refs[n_a]
    rest = list(refs[n_a + 1:])
    x_ref, g_ref = (rest.pop(0), rest.pop(0)) if mode == "resid" else (None, None)
    side_in = rest.pop(0) if side else None
    o_ref = rest.pop(0)
    side_out = rest.pop(0) if side else None
    wf_s, wb_s, sem = rest
    j = pl.program_id(0)

    if side:
        side_out[...] = side_in[...].astype(side_out.dtype)

    def tile_copy(jj):
        sel = pl.ds(pl.multiple_of(jj * tn, tn), tn)
        src = w_hbm.at[layer, sel, :] if w_t else w_hbm.at[layer, :, sel]
        return pltpu.make_async_copy(src, wf_s, sem)

    @pl.when(pl.program_id(1) == 0)
    def _():
        @pl.when(j == 0)
        def _():
            tile_copy(0).start()

        tile_copy(j).wait()
        n_rows = wf_s.shape[0]
        for r0 in range(0, n_rows, W_CAST_ROWS):
            r1 = min(r0 + W_CAST_ROWS, n_rows)
            wb_s[r0:r1, :] = wf_s[r0:r1, :].astype(BF16)

        @pl.when(j + 1 < n_col_tiles)
        def _():
            tile_copy(j + 1).start()

    _mm_epilogue(mode, _slab_dots(a_refs, wb_s, w_t), o_ref, x_ref, g_ref)


def _matmul_wcast(a_list, w_stack, layer, *, mode, out_dtype, tm, tn, n_cols=None, w_t=False,
                  resid=None, rows_per_batch=None, side_cast=None):
    m = a_list[0].shape[0]
    if w_t:
        _, n_total, kdim = w_stack.shape
    else:
        _, kdim, n_total = w_stack.shape
    n = n_total if n_cols is None else n_cols
    n_a = len(a_list)
    tm = _fit_tile(tm, m)
    tn = _fit_tile(tn, n)
    w_tile = (tn, kdim) if w_t else (kdim, tn)
    assert m % tm == 0 and n % tn == 0
    assert sum(a.shape[1] for a in a_list) == kdim

    in_specs = [pl.BlockSpec((tm, a.shape[1]), lambda j, i: (i, 0)) for a in a_list]
    in_specs.append(pl.BlockSpec(memory_space=pl.ANY))
    args = list(a_list) + [w_stack]
    if mode == "resid":
        x2, mod6, gate_idx = resid
        assert rows_per_batch % tm == 0
        tiles_per_batch = rows_per_batch // tm
        in_specs.append(pl.BlockSpec((tm, tn), lambda j, i: (i, j)))
        in_specs.append(pl.BlockSpec(
            (None, 1, tn), lambda j, i: ((i // tiles_per_batch) * N_MOD + gate_idx, 0, j)))
        args += [x2, mod6]
    out_specs = [pl.BlockSpec((tm, tn), lambda j, i: (i, j))]
    out_shape = [jax.ShapeDtypeStruct((m, n), out_dtype)]
    n_i = m // tm
    if side_cast is not None:
        side_stack, side_layer = side_cast
        _, s_rows, s_cols = side_stack.shape
        n_steps = (n // tn) * n_i
        rs = s_rows // n_steps
        assert rs * n_steps == s_rows and rs % 16 == 0
        in_specs.append(pl.BlockSpec((None, rs, s_cols), lambda j, i: (side_layer, j * n_i + i, 0)))
        args.append(side_stack)
        out_specs.append(pl.BlockSpec((rs, s_cols), lambda j, i: (j * n_i + i, 0)))
        out_shape.append(jax.ShapeDtypeStruct((s_rows, s_cols), BF16))
    outs = pl.pallas_call(
        functools.partial(_mm_wcast_kernel, n_a=n_a, mode=mode, layer=layer,
                          n_col_tiles=n // tn, tn=tn, w_t=w_t, side=side_cast is not None),
        grid=(n // tn, n_i),
        in_specs=in_specs,
        out_specs=out_specs,
        out_shape=out_shape,
        scratch_shapes=[pltpu.VMEM(w_tile, F32), pltpu.VMEM(w_tile, BF16),
                        pltpu.SemaphoreType.DMA(())],
        compiler_params=_params("arbitrary", "arbitrary"),
        name="mmw_" + mode,
    )(*args)
    return outs if side_cast is not None else outs[0]


def _mm_kernel(*refs, n_a, nk, mode):
    a_refs = refs[:n_a]
    w_ref = refs[n_a]
    if mode == "resid":
        x_ref, g_ref, o_ref = refs[n_a + 1:]
    else:
        o_ref, = refs[n_a + 1:]
        x_ref = g_ref = None

    part = _slab_dots(a_refs, w_ref)

    def finish(acc):
        _mm_epilogue(mode, acc, o_ref, x_ref, g_ref)

    if nk == 1:
        finish(part)
    else:
        k = pl.program_id(2)

        @pl.when(k == 0)
        def _():
            o_ref[...] = part

        if nk > 2:
            @pl.when(jnp.logical_and(k > 0, k < nk - 1))
            def _():
                o_ref[...] += part

        @pl.when(k == nk - 1)
        def _():
            finish(o_ref[...] + part)


def _matmul(a_list, w_stack, layer, *, mode, out_dtype, tm, tn, tk=None, n_cols=None,
            resid=None, rows_per_batch=None):
    m = a_list[0].shape[0]
    _, kdim, n_total = w_stack.shape
    n = n_total if n_cols is None else n_cols
    n_a = len(a_list)
    tm = _fit_tile(tm, m)
    tn = _fit_tile(tn, n)
    if tk is None or tk >= kdim:
        tk = kdim
    nk = kdim // tk
    assert n_a == 1 or nk == 1
    assert nk == 1 or out_dtype == F32
    assert m % tm == 0 and n % tn == 0 and kdim % tk == 0

    in_specs = []
    if n_a == 1:
        in_specs.append(pl.BlockSpec((tm, tk), lambda i, j, k: (i, k)))
    else:
        for a in a_list:
            in_specs.append(pl.BlockSpec((tm, a.shape[1]), lambda i, j, k: (i, 0)))
    in_specs.append(pl.BlockSpec((None, tk, tn), lambda i, j, k: (layer, k, j)))
    args = list(a_list) + [w_stack]
    if mode == "resid":
        x2, mod6, gate_idx = resid
        assert rows_per_batch % tm == 0
        tiles_per_batch = rows_per_batch // tm
        in_specs.append(pl.BlockSpec((tm, tn), lambda i, j, k: (i, j)))
        in_specs.append(pl.BlockSpec(
            (None, 1, tn), lambda i, j, k: ((i // tiles_per_batch) * N_MOD + gate_idx, 0, j)))
        args += [x2, mod6]
    return pl.pallas_call(
        functools.partial(_mm_kernel, n_a=n_a, nk=nk, mode=mode),
        grid=(m // tm, n // tn, nk),
        in_specs=in_specs,
        out_specs=pl.BlockSpec((tm, tn), lambda i, j, k: (i, j)),
        out_shape=jax.ShapeDtypeStruct((m, n), out_dtype),
        compiler_params=_params("arbitrary", "arbitrary", "arbitrary"),
        name="mm_" + mode,
    )(*args)


def _fcum_kernel(fg_ref, bias_ref, fc_ref):
    n_slab = fg_ref.shape[0] // HEAD_DIM
    tril = _tril(HEAD_DIM)

    def body(r, carry):
        rows = pl.ds(pl.multiple_of(r * HEAD_DIM, HEAD_DIM), HEAD_DIM)
        z = fg_ref[rows, :] + bias_ref[...]
        lf = jnp.minimum(z, 0.0) - jnp.log1p(jnp.exp(-jnp.abs(z)))
        f = _tril_cumsum(tril, lf) + carry
        fc_ref[rows, :] = f
        return f[HEAD_DIM - 1:HEAD_DIM, :]

    lax.fori_loop(0, n_slab, body, jnp.zeros((1, HEAD_DIM), F32))


def _fcum(fg3, bias_row):
    b, s, _ = fg3.shape
    return pl.pallas_call(
        _fcum_kernel,
        grid=(b,),
        in_specs=[
            pl.BlockSpec((None, s, HEAD_DIM), lambda i: (i, 0, 0)),
            pl.BlockSpec((1, HEAD_DIM), lambda i: (0, 0)),
        ],
        out_specs=pl.BlockSpec((None, s, HEAD_DIM), lambda i: (i, 0, 0)),
        out_shape=jax.ShapeDtypeStruct((b, s, HEAD_DIM), F32),
        compiler_params=_params("arbitrary"),
        name="fox_fcum",
    )(fg3, bias_row)


V_ROWS = HEAD_DIM + 16


def _fox_kernel(q_ref, k_ref, v_ref, fc_ref, qn_ref, kn_ref, o_ref,
                kaug_s, vaug_s, qaug_s, s0_s, s1_s, m_s, acc_s, *, blk, n_blk):
    h = pl.program_id(1)
    qi = pl.program_id(2)
    qblk = 2 * blk
    lane = lax.broadcasted_iota(jnp.int32, (blk, HEAD_DIM), 1)

    @pl.when(qi == 0)
    def _():
        ones_row = (lax.broadcasted_iota(jnp.int32, (V_ROWS - HEAD_DIM, blk), 0) == 0)
        for c in range(n_blk):
            rows = slice(c * blk, (c + 1) * blk)
            k = k_ref[rows, :]
            ms = jnp.mean(k * k, axis=-1, keepdims=True)
            kaug_s[rows, 0:HEAD_DIM] = (k * lax.rsqrt(ms + EPS) * kn_ref[...]).astype(BF16)
            fcol = jnp.sum(jnp.where(lane == h, fc_ref[rows, :], 0.0), axis=1, keepdims=True)
            nf = fcol * (-LOG2E)
            hi = nf.astype(BF16).astype(F32)
            mid = (nf - hi).astype(BF16).astype(F32)
            lo = (nf - hi) - mid
            aug = jnp.where(lane == 0, hi, jnp.where(lane == 1, mid, jnp.where(lane == 2, lo, 0.0)))
            kaug_s[rows, HEAD_DIM:2 * HEAD_DIM] = aug.astype(BF16)
            vaug_s[c, 0:HEAD_DIM, :] = v_ref[rows, :].T.astype(BF16)
            vaug_s[c, HEAD_DIM:V_ROWS, :] = ones_row.astype(BF16)

    q = q_ref[...]
    ms = jnp.mean(q * q, axis=-1, keepdims=True)
    qaug_s[:, 0:HEAD_DIM] = (q * lax.rsqrt(ms + EPS) * qn_ref[...]
                             * (HEAD_DIM ** -0.5 * LOG2E)).astype(BF16)
    qaug_s[:, HEAD_DIM:2 * HEAD_DIM] = (
        lax.broadcasted_iota(jnp.int32, (qblk, HEAD_DIM), 1) < 3).astype(BF16)
    m_s[...] = jnp.full(m_s.shape, MASK_VALUE, F32)
    acc_s[...] = jnp.zeros(acc_s.shape, F32)

    def scores(kj, q0=0):
        rows = pl.ds(pl.multiple_of(kj * blk, blk), blk)
        return lax.dot_general(kaug_s[rows, :], qaug_s[q0:, :], _NT, preferred_element_type=F32)

    def update(s_ref, kj, causal, q0=0):
        st = s_ref[:, q0:]
        if causal:
            kr = lax.broadcasted_iota(jnp.int32, st.shape, 0)
            qc = lax.broadcasted_iota(jnp.int32, st.shape, 1)
            st = jnp.where(kr <= qc, st, MASK_VALUE)
        m_old = m_s[:, q0:]
        m_new = jnp.maximum(m_old, jnp.max(st, axis=0, keepdims=True))
        pt = jnp.exp2(st - m_new).astype(BF16)
        alpha = jnp.exp2(m_old - m_new)
        acc_s[:, q0:] = alpha * acc_s[:, q0:] + jnp.dot(vaug_s[kj], pt,
                                                        preferred_element_type=F32)
        m_s[:, q0:] = m_new

    s0_s[...] = scores(0)

    def pair(i, carry):
        kj = 2 * i
        s1_s[...] = scores(kj + 1)
        update(s0_s, kj, False)
        s0_s[...] = scores(kj + 2)
        update(s1_s, kj + 1, False)
        return carry

    lax.fori_loop(0, qi, pair, 0)
    s1_s[:, blk:] = scores(2 * qi + 1, blk)
    update(s0_s, 2 * qi, True)
    update(s1_s, 2 * qi + 1, True, blk)

    acc = acc_s[...]
    ot = acc[0:HEAD_DIM, :] / acc[HEAD_DIM:HEAD_DIM + 1, :]
    o_ref[...] = ot.T.astype(o_ref.dtype)


def _fox(proj3, fc, qn_w, kn_w, *, n_heads, q_blk0, k_blk0, v_blk0, blk):
    b, s, _ = proj3.shape
    n_blk = s // blk
    qblk = 2 * blk
    assert s % qblk == 0
    return pl.pallas_call(
        functools.partial(_fox_kernel, blk=blk, n_blk=n_blk),
        grid=(b, n_heads, s // qblk),
        in_specs=[
            pl.BlockSpec((None, qblk, HEAD_DIM), lambda i, h, t: (i, t, q_blk0 + h)),
            pl.BlockSpec((None, s, HEAD_DIM), lambda i, h, t: (i, 0, k_blk0 + h)),
            pl.BlockSpec((None, s, HEAD_DIM), lambda i, h, t: (i, 0, v_blk0 + h)),
            pl.BlockSpec((None, s, HEAD_DIM), lambda i, h, t: (i, 0, 0)),
            pl.BlockSpec((1, HEAD_DIM), lambda i, h, t: (0, 0)),
            pl.BlockSpec((1, HEAD_DIM), lambda i, h, t: (0, 0)),
        ],
        out_specs=pl.BlockSpec((None, qblk, HEAD_DIM), lambda i, h, t: (i, t, h)),
        out_shape=jax.ShapeDtypeStruct((b, s, n_heads * HEAD_DIM), BF16),
        scratch_shapes=[
            pltpu.VMEM((s, 2 * HEAD_DIM), BF16),
            pltpu.VMEM((n_blk, V_ROWS, blk), BF16),
            pltpu.VMEM((qblk, 2 * HEAD_DIM), BF16),
            pltpu.VMEM((blk, qblk), F32),
            pltpu.VMEM((blk, qblk), F32),
            pltpu.VMEM((1, qblk), F32),
            pltpu.VMEM((V_ROWS, qblk), F32),
        ],
        compiler_params=_params("arbitrary", "arbitrary", "arbitrary"),
        name="fox_attn",
    )(proj3, proj3, proj3, fc, qn_w.reshape(1, HEAD_DIM), kn_w.reshape(1, HEAD_DIM))


def _hgrn2_kernel(q_ref, f_ref, i_ref, g_ref, lb_ref, nw_ref, seg_ref, o_ref,
                  st_s, y_s, g_s, kk_s, qf_s, r_s, a_s, qe_s, u_s, dec_s, *, layer, rows):
    t_id = pl.program_id(2)
    n_chunk = rows // CHUNK

    @pl.when(t_id == 0)
    def _():
        st_s[...] = jnp.zeros(st_s.shape, F32)

    depth = lb_ref.shape[0]
    lbs = [lb_ref[j] for j in range(depth)]
    mx = lbs[0]
    for j in range(1, depth):
        mx = jnp.maximum(mx, lbs[j])
    es = [jnp.exp(v - mx) for v in lbs]
    den = es[0]
    for j in range(1, depth):
        den = den + es[j]
    lb = jnp.zeros((1, HEAD_DIM), F32)
    for j in range(1, layer + 1):
        lb = lb + es[j] / den

    tril = _tril(CHUNK)

    half = SUB // 2
    zeros_half = jnp.zeros((half, HEAD_DIM), F32)
    row_half = lax.broadcasted_iota(jnp.int32, (half, HEAD_DIM), 0)

    def prep(c, carry):
        rws = pl.ds(pl.multiple_of(c * CHUNK, CHUNK), CHUNK)
        z = f_ref[rws, :]
        kk = (1.0 - lb) * _sigmoid(-z)
        logf = jnp.log1p(-jnp.minimum(kk, K_MAX))
        gc = _tril_cumsum(tril, logf) * LOG2E
        q = q_ref[rws, :]
        qf = q * _sigmoid(q)
        kk_s[rws, :] = kk
        g_s[rws, :] = gc
        qf_s[rws, :] = qf
        for b in range(N_SUB):
            r0 = b * SUB
            q_t, q_b = qf[r0:r0 + half], qf[r0 + half:r0 + SUB]
            k_t, k_b = kk[r0:r0 + half], kk[r0 + half:r0 + SUB]
            g_t, g_b = gc[r0:r0 + half], gc[r0 + half:r0 + SUB]
            m_t = g_t[half // 2:half // 2 + 1]
            m_b = g_b[half // 2:half // 2 + 1]
            g_8 = g_b[0:1]
            qt_f = q_t * jnp.exp2(g_t - m_t)
            pt_f = k_t * jnp.exp2(m_t - g_t)
            qb_f = q_b * jnp.exp2(g_b - m_b)
            pb_f = k_b * jnp.exp2(m_b - g_b)
            qx_f = q_b * jnp.exp2(g_b - g_8)
            px_f = k_t * jnp.exp2(g_8 - g_t)
            yrows = pl.ds(pl.multiple_of(c * CHUNK + r0, SUB), SUB)
            for s in range(SUB):
                if s < half:
                    top = jnp.where(row_half >= s, qt_f * pt_f[s:s + 1], 0.0)
                    bot = qx_f * px_f[s:s + 1]
                else:
                    top = zeros_half
                    bot = jnp.where(row_half >= s - half, qb_f * pb_f[s - half:s - half + 1], 0.0)
                x = jnp.concatenate([top, bot], axis=0)
                y_s[yrows, s * HEAD_DIM:(s + 1) * HEAD_DIM] = x.astype(BF16)
        return carry

    lax.fori_loop(0, n_chunk, prep, 0, unroll=True)

    r_s[...] = jnp.dot(y_s[...], seg_ref[...], preferred_element_type=F32)

    tr = lax.broadcasted_iota(jnp.int32, (CHUNK, CHUNK), 0)
    tc = lax.broadcasted_iota(jnp.int32, (CHUNK, CHUNK), 1)
    same_blk = (tr // SUB) == (tc // SUB)
    diag_mask = jnp.logical_and(same_blk, tc <= tr)

    def scores(c, carry):
        rws = pl.ds(pl.multiple_of(c * CHUNK, CHUNK), CHUNK)
        qf = qf_s[rws, :]
        kk = kk_s[rws, :]
        gc = g_s[rws, :]
        blocks = [jnp.zeros((SUB, CHUNK), F32)]
        for i in range(1, N_SUB):
            lo = i * SUB
            gn = gc[lo:lo + 1]
            ql = qf[lo:lo + SUB] * jnp.exp2(gc[lo:lo + SUB] - gn)
            kr = jnp.concatenate([kk[:lo] * jnp.exp2(gn - gc[:lo]),
                                  jnp.zeros((CHUNK - lo, HEAD_DIM), F32)], axis=0)
            blocks.append(lax.dot_general(ql.astype(BF16), kr.astype(BF16), _NT,
                                          preferred_element_type=F32))
        a_off = jnp.concatenate(blocks, axis=0)
        a = jnp.where(diag_mask, r_s[rws, :][:, :CHUNK], a_off)
        a_s[rws, :] = a.astype(BF16)
        g_last = gc[CHUNK - 1:CHUNK]
        qe_s[rws, :] = (qf * jnp.exp2(gc)).astype(BF16)
        kd = kk * jnp.exp2(g_last - gc)
        u_s[c] = lax.dot_general(i_ref[rws, :].astype(BF16), kd.astype(BF16), _TN,
                                 preferred_element_type=F32)
        dec_s[c] = jnp.exp2(g_last)
        return carry

    lax.fori_loop(0, n_chunk, scores, 0, unroll=True)

    def chunk(c, carry):
        rws = pl.ds(pl.multiple_of(c * CHUNK, CHUNK), CHUNK)
        vb = i_ref[rws, :].astype(BF16)
        st = st_s[...]
        o = jnp.dot(a_s[rws, :], vb, preferred_element_type=F32)
        o = o + lax.dot_general(qe_s[rws, :], st.astype(BF16), _NT,
                                preferred_element_type=F32)
        st_s[...] = dec_s[c] * st + u_s[c]
        ms = jnp.mean(o * o, axis=-1, keepdims=True)
        y = o * lax.rsqrt(ms + EPS) * nw_ref[...]
        g = g_ref[rws, :]
        o_ref[rws, :] = (y * (g * _sigmoid(g))).astype(o_ref.dtype)
        return carry

    lax.fori_loop(0, n_chunk, chunk, 0, unroll=True)


def _hgrn2(proj3, lower_bounds, nw, *, layer, n_heads, rows):
    b, s, _ = proj3.shape
    depth = lower_bounds.shape[0]
    rows = min(rows, s)
    seg_r = jnp.arange(SUB * HEAD_DIM, dtype=jnp.int32)[:, None] // HEAD_DIM
    seg_c = jnp.arange(HEAD_DIM, dtype=jnp.int32)[None, :]
    seg = jnp.logical_and(seg_c < CHUNK, (seg_c % SUB) == seg_r).astype(BF16)
    lb4 = lower_bounds.astype(F32).reshape(depth, n_heads, 1, HEAD_DIM)

    def col(off):
        return pl.BlockSpec((None, rows, HEAD_DIM), lambda i, h, t: (i, t, off * n_heads + h))

    return pl.pallas_call(
        functools.partial(_hgrn2_kernel, layer=layer, rows=rows),
        grid=(b, n_heads, s // rows),
        in_specs=[
            col(0), col(1), col(2), col(3),
            pl.BlockSpec((depth, None, 1, HEAD_DIM), lambda i, h, t: (0, h, 0, 0)),
            pl.BlockSpec((1, HEAD_DIM), lambda i, h, t: (0, 0)),
            pl.BlockSpec((SUB * HEAD_DIM, HEAD_DIM), lambda i, h, t: (0, 0)),
        ],
        out_specs=pl.BlockSpec((None, rows, HEAD_DIM), lambda i, h, t: (i, t, h)),
        out_shape=jax.ShapeDtypeStruct((b, s, n_heads * HEAD_DIM), BF16),
        scratch_shapes=[
            pltpu.VMEM((HEAD_DIM, HEAD_DIM), F32),
            pltpu.VMEM((rows, SUB * HEAD_DIM), BF16),
            pltpu.VMEM((rows, HEAD_DIM), F32),
            pltpu.VMEM((rows, HEAD_DIM), F32),
            pltpu.VMEM((rows, HEAD_DIM), F32),
            pltpu.VMEM((rows, HEAD_DIM), F32),
            pltpu.VMEM((rows, CHUNK), BF16),
            pltpu.VMEM((rows, HEAD_DIM), BF16),
            pltpu.VMEM((rows // CHUNK, HEAD_DIM, HEAD_DIM), F32),
            pltpu.VMEM((rows // CHUNK, 1, HEAD_DIM), F32),
        ],
        compiler_params=_params("arbitrary", "arbitrary", "arbitrary"),
        name="hgrn2",
    )(proj3, proj3, proj3, proj3, lb4, nw.reshape(1, HEAD_DIM), seg)


def kernel(x, c, lower_bounds, w_ada, b_ada, norm_mix_w, norm_ffn_w, w_in, rec_norm_w,
           fg_bias, q_norm_w, k_norm_w, w_out, w_up, w_down):
    b, s, d = x.shape
    depth = w_ada.shape[0]
    n_heads = d // HEAD_DIM
    h_rec = n_heads // 2
    h_att = n_heads - h_rec
    main_cols = (4 * h_rec + 3 * h_att) * HEAD_DIM
    assert w_in.shape[2] == main_cols + h_att and h_att <= HEAD_DIM
    assert s % CHUNK == 0 and s % HEAD_DIM == 0
    m = b * s
    fox_blk = min(FOX_KEY_BLK, s // 2)

    rows_pad = -(-b // 8) * 8
    c_pad = jnp.zeros((rows_pad, d), F32).at[:b].set(c.astype(F32))
    mod = _adaln_mod(c_pad, w_ada, b_ada)

    w_in_t = jnp.swapaxes(w_in, 1, 2)

    x2 = x.reshape(m, d)
    for l in range(depth):
        mod6 = mod[l, :b].reshape(b * N_MOD, 1, d)

        h, fg3 = _norm_mod(x2.reshape(b, s, d), norm_mix_w[l], mod6, 1, 0,
                           fg_weights=(w_in_t, l, main_cols))
        proj = _matmul_wcast([h.reshape(m, d)], w_in_t, l, mode="plain", out_dtype=F32,
                             tm=MM_TM, tn=MM_TN, n_cols=main_cols, w_t=True)
        proj3 = proj.reshape(b, s, main_cols)

        o_rec = _hgrn2(proj3, lower_bounds, rec_norm_w[l], layer=l, n_heads=h_rec,
                       rows=HGRN2_ROWS)

        bias_row = jnp.pad(fg_bias[l].astype(F32), (0, HEAD_DIM - h_att)).reshape(1, HEAD_DIM)
        fc = _fcum(fg3, bias_row)
        o_att = _fox(proj3, fc, q_norm_w[l], k_norm_w[l], n_heads=h_att,
                     q_blk0=4 * h_rec, k_blk0=4 * h_rec + h_att, v_blk0=4 * h_rec + 2 * h_att,
                     blk=fox_blk)

        x2 = _matmul_wcast([o_rec.reshape(m, -1), o_att.reshape(m, -1)], w_out, l, mode="resid",
                           out_dtype=F32, tm=MM_TM, tn=W_OUT_TN, resid=(x2, mod6, 2),
                           rows_per_batch=s)

        h2 = _norm_mod(x2.reshape(b, s, d), norm_ffn_w[l], mod6, 4, 3).reshape(m, d)
        u, w_down_b = _matmul_wcast([h2], w_up, l, mode="relu2", out_dtype=BF16, tm=MM_TM,
                                    tn=MM_TN, side_cast=(w_down, l))
        x2 = _matmul([u], w_down_b[None], 0, mode="resid", out_dtype=F32, tm=MM_TM, tn=MM_TN,
                     tk=W_DOWN_TK, resid=(x2, mod6, 5), rows_per_batch=s)
    return x2.reshape(b, s, d)
```

```python
import functools

import jax
import jax.numpy as jnp
from jax import lax
from jax.experimental import pallas as pl
from jax.experimental.pallas import tpu as pltpu

HEAD_DIM = 128
CHUNK = 64
SUB = 16
N_SUB = CHUNK // SUB
N_MOD = 6
EPS = 1e-6
MASK_VALUE = -1e30
K_MAX = 1.0 - 1e-6
LOG2E = 1.4426950408889634
VMEM_LIMIT_BYTES = 60 * 1024 * 1024

MM_TM = 1024
MM_TN = 1024
W_OUT_TN = 512
W_DOWN_TK = 4096
ADALN_TN = 512
NORM_ROWS = 512
FOX_KEY_BLK = 512
HGRN2_ROWS = 4096

F32 = jnp.float32
BF16 = jnp.bfloat16

_NT = (((1,), (1,)), ((), ()))
_TN = (((0,), (0,)), ((), ()))


def _params(*sem):
    return pltpu.CompilerParams(dimension_semantics=sem, vmem_limit_bytes=VMEM_LIMIT_BYTES)


def _fit_tile(tile, dim):
    tile = min(tile, dim)
    while dim % tile:
        tile -= HEAD_DIM
    return tile


def _sigmoid(x):
    return 1.0 / (1.0 + jnp.exp(-x))


def _tril_cumsum(tril, x):
    hi = x.astype(BF16)
    r1 = x - hi.astype(F32)
    mid = r1.astype(BF16)
    lo = (r1 - mid.astype(F32)).astype(BF16)
    out = jnp.dot(tril, hi, preferred_element_type=F32)
    out = out + jnp.dot(tril, mid, preferred_element_type=F32)
    return out + jnp.dot(tril, lo, preferred_element_type=F32)


def _tril(n):
    r = lax.broadcasted_iota(jnp.int32, (n, n), 0)
    c = lax.broadcasted_iota(jnp.int32, (n, n), 1)
    return (r >= c).astype(BF16)


def _mod_kernel(c_ref, w_ref, b_ref, o_ref):
    c = c_ref[...]
    ca = (c * _sigmoid(c)).astype(BF16)
    w = w_ref[...].astype(BF16)
    o_ref[...] = jnp.dot(ca, w, preferred_element_type=F32) + b_ref[...]


def _adaln_mod(c_pad, w_ada, b_ada):
    depth, d, n = w_ada.shape
    rows = c_pad.shape[0]
    tn = _fit_tile(ADALN_TN, n)
    return pl.pallas_call(
        _mod_kernel,
        grid=(depth, n // tn),
        in_specs=[
            pl.BlockSpec((rows, d), lambda l, j: (0, 0)),
            pl.BlockSpec((None, d, tn), lambda l, j: (l, 0, j)),
            pl.BlockSpec((None, 1, tn), lambda l, j: (l, 0, j)),
        ],
        out_specs=pl.BlockSpec((None, rows, tn), lambda l, j: (l, 0, j)),
        out_shape=jax.ShapeDtypeStruct((depth, rows, n), F32),
        compiler_params=_params("arbitrary", "arbitrary"),
        name="adaln_mod",
    )(c_pad, w_ada, b_ada.reshape(depth, 1, n))


def _norm_mod_kernel(x_ref, w_ref, scale_ref, shift_ref, *rest, n_valid):
    x = x_ref[...]
    ms = jnp.mean(x * x, axis=-1, keepdims=True)
    y = x * lax.rsqrt(ms + EPS) * w_ref[...]
    h = (y * (1.0 + scale_ref[...]) + shift_ref[...]).astype(BF16)
    if n_valid is None:
        o_ref, = rest
    else:
        wfg_ref, o_ref, fg_ref = rest
        row = lax.broadcasted_iota(jnp.int32, wfg_ref.shape, 0)
        wfg = jnp.where(row < n_valid, wfg_ref[...], 0.0).astype(BF16)
        fg_ref[...] = lax.dot_general(h, wfg, _NT, preferred_element_type=F32)
    o_ref[...] = h


def _norm_mod(x3, w, mod6, scale_idx, shift_idx, fg_weights=None):
    b, s, d = x3.shape
    ts = min(NORM_ROWS, s)
    in_specs = [
        pl.BlockSpec((None, ts, d), lambda i, t: (i, t, 0)),
        pl.BlockSpec((1, d), lambda i, t: (0, 0)),
        pl.BlockSpec((None, 1, d), lambda i, t: (i * N_MOD + scale_idx, 0, 0)),
        pl.BlockSpec((None, 1, d), lambda i, t: (i * N_MOD + shift_idx, 0, 0)),
    ]
    args = [x3, w.reshape(1, d), mod6, mod6]
    out_specs = [pl.BlockSpec((None, ts, d), lambda i, t: (i, t, 0))]
    out_shape = [jax.ShapeDtypeStruct((b, s, d), BF16)]
    n_valid = None
    if fg_weights is not None:
        w_in_t, layer, row0 = fg_weights
        n_valid = w_in_t.shape[1] - row0
        assert row0 % HEAD_DIM == 0 and 0 < n_valid <= HEAD_DIM and w_in_t.shape[2] == d
        in_specs.append(pl.BlockSpec((None, HEAD_DIM, d),
                                     lambda i, t: (layer, row0 // HEAD_DIM, 0)))
        args.append(w_in_t)
        out_specs.append(pl.BlockSpec((None, ts, HEAD_DIM), lambda i, t: (i, t, 0)))
        out_shape.append(jax.ShapeDtypeStruct((b, s, HEAD_DIM), F32))
    outs = pl.pallas_call(
        functools.partial(_norm_mod_kernel, n_valid=n_valid),
        grid=(b, s // ts),
        in_specs=in_specs,
        out_specs=out_specs,
        out_shape=out_shape,
        compiler_params=_params("arbitrary", "arbitrary"),
        name="norm_mod",
    )(*args)
    return outs if fg_weights is not None else outs[0]


def _slab_dots(a_refs, w_ref, w_t=False):
    part = None
    off = 0
    for a_ref in a_refs:
        ka = a_ref.shape[1]
        if w_t:
            p = lax.dot_general(a_ref[...], w_ref[:, off:off + ka], _NT,
                                preferred_element_type=F32)
        else:
            p = jnp.dot(a_ref[...], w_ref[off:off + ka, :], preferred_element_type=F32)
        part = p if part is None else part + p
        off += ka
    return part


def _mm_epilogue(mode, acc, o_ref, x_ref, g_ref):
    if mode == "plain":
        o_ref[...] = acc.astype(o_ref.dtype)
    elif mode == "relu2":
        r = jnp.maximum(acc, 0.0)
        o_ref[...] = (r * r).astype(o_ref.dtype)
    else:
        o_ref[...] = x_ref[...] + g_ref[...] * acc


W_CAST_ROWS = 512


def _mm_wcast_kernel(*refs, n_a, mode, layer, n_col_tiles, tn, w_t, side):
    a_refs = refs[:n_a]
    w_hbm = refs[n_a]
    rest = list(refs[n_a + 1:])
    x_ref, g_ref = (rest.pop(0), rest.pop(0)) if mode == "resid" else (None, None)
    side_in = rest.pop(0) if side else None
    o_ref = rest.pop(0)
    side_out = rest.pop(0) if side else None
    wf_s, wb_s, sem = rest
    j = pl.program_id(0)

    if side:
        side_out[...] = side_in[...].astype(side_out.dtype)

    def tile_copy(jj):
        sel = pl.ds(pl.multiple_of(jj * tn, tn), tn)
        src = w_hbm.at[layer, sel, :] if w_t else w_hbm.at[layer, :, sel]
        return pltpu.make_async_copy(src, wf_s, sem)

    @pl.when(pl.program_id(1) == 0)
    def _():
        @pl.when(j == 0)
        def _():
            tile_copy(0).start()

        tile_copy(j).wait()
        n_rows = wf_s.shape[0]
        for r0 in range(0, n_rows, W_CAST_ROWS):
            r1 = min(r0 + W_CAST_ROWS, n_rows)
            wb_s[r0:r1, :] = wf_s[r0:r1, :].astype(BF16)

        @pl.when(j + 1 < n_col_tiles)
        def _():
            tile_copy(j + 1).start()

    _mm_epilogue(mode, _slab_dots(a_refs, wb_s, w_t), o_ref, x_ref, g_ref)


def _matmul_wcast(a_list, w_stack, layer, *, mode, out_dtype, tm, tn, n_cols=None, w_t=False,
                  resid=None, rows_per_batch=None, side_cast=None):
    m = a_list[0].shape[0]
    if w_t:
        _, n_total, kdim = w_stack.shape
    else:
        _, kdim, n_total = w_stack.shape
    n = n_total if n_cols is None else n_cols
    n_a = len(a_list)
    tm = _fit_tile(tm, m)
    tn = _fit_tile(tn, n)
    w_tile = (tn, kdim) if w_t else (kdim, tn)
    assert m % tm == 0 and n % tn == 0
    assert sum(a.shape[1] for a in a_list) == kdim

    in_specs = [pl.BlockSpec((tm, a.shape[1]), lambda j, i: (i, 0)) for a in a_list]
    in_specs.append(pl.BlockSpec(memory_space=pl.ANY))
    args = list(a_list) + [w_stack]
    if mode == "resid":
        x2, mod6, gate_idx = resid
        assert rows_per_batch % tm == 0
        tiles_per_batch = rows_per_batch // tm
        in_specs.append(pl.BlockSpec((tm, tn), lambda j, i: (i, j)))
        in_specs.append(pl.BlockSpec(
            (None, 1, tn), lambda j, i: ((i // tiles_per_batch) * N_MOD + gate_idx, 0, j)))
        args += [x2, mod6]
    out_specs = [pl.BlockSpec((tm, tn), lambda j, i: (i, j))]
    out_shape = [jax.ShapeDtypeStruct((m, n), out_dtype)]
    n_i = m // tm
    if side_cast is not None:
        side_stack, side_layer = side_cast
        _, s_rows, s_cols = side_stack.shape
        n_steps = (n // tn) * n_i
        rs = s_rows // n_steps
        assert rs * n_steps == s_rows and rs % 16 == 0
        in_specs.append(pl.BlockSpec((None, rs, s_cols), lambda j, i: (side_layer, j * n_i + i, 0)))
        args.append(side_stack)
        out_specs.append(pl.BlockSpec((rs, s_cols), lambda j, i: (j * n_i + i, 0)))
        out_shape.append(jax.ShapeDtypeStruct((s_rows, s_cols), BF16))
    outs = pl.pallas_call(
        functools.partial(_mm_wcast_kernel, n_a=n_a, mode=mode, layer=layer,
                          n_col_tiles=n // tn, tn=tn, w_t=w_t, side=side_cast is not None),
        grid=(n // tn, n_i),
        in_specs=in_specs,
        out_specs=out_specs,
        out_shape=out_shape,
        scratch_shapes=[pltpu.VMEM(w_tile, F32), pltpu.VMEM(w_tile, BF16),
                        pltpu.SemaphoreType.DMA(())],
        compiler_params=_params("arbitrary", "arbitrary"),
        name="mmw_" + mode,
    )(*args)
    return outs if side_cast is not None else outs[0]


def _mm_kernel(*refs, n_a, nk, mode):
    a_refs = refs[:n_a]
    w_ref = refs[n_a]
    if mode == "resid":
        x_ref, g_ref, o_ref = refs[n_a + 1:]
    else:
        o_ref, = refs[n_a + 1:]
        x_ref = g_ref = None

    part = _slab_dots(a_refs, w_ref)

    def finish(acc):
        _mm_epilogue(mode, acc, o_ref, x_ref, g_ref)

    if nk == 1:
        finish(part)
    else:
        k = pl.program_id(2)

        @pl.when(k == 0)
        def _():
            o_ref[...] = part

        if nk > 2:
            @pl.when(jnp.logical_and(k > 0, k < nk - 1))
            def _():
                o_ref[...] += part

        @pl.when(k == nk - 1)
        def _():
            finish(o_ref[...] + part)


def _matmul(a_list, w_stack, layer, *, mode, out_dtype, tm, tn, tk=None, n_cols=None,
            resid=None, rows_per_batch=None):
    m = a_list[0].shape[0]
    _, kdim, n_total = w_stack.shape
    n = n_total if n_cols is None else n_cols
    n_a = len(a_list)
    tm = _fit_tile(tm, m)
    tn = _fit_tile(tn, n)
    if tk is None or tk >= kdim:
        tk = kdim
    nk = kdim // tk
    assert n_a == 1 or nk == 1
    assert nk == 1 or out_dtype == F32
    assert m % tm == 0 and n % tn == 0 and kdim % tk == 0

    in_specs = []
    if n_a == 1:
        in_specs.append(pl.BlockSpec((tm, tk), lambda i, j, k: (i, k)))
    else:
        for a in a_list:
            in_specs.append(pl.BlockSpec((tm, a.shape[1]), lambda i, j, k: (i, 0)))
    in_specs.append(pl.BlockSpec((None, tk, tn), lambda i, j, k: (layer, k, j)))
    args = list(a_list) + [w_stack]
    if mode == "resid":
        x2, mod6, gate_idx = resid
        assert rows_per_batch % tm == 0
        tiles_per_batch = rows_per_batch // tm
        in_specs.append(pl.BlockSpec((tm, tn), lambda i, j, k: (i, j)))
        in_specs.append(pl.BlockSpec(
            (None, 1, tn), lambda i, j, k: ((i // tiles_per_batch) * N_MOD + gate_idx, 0, j)))
        args += [x2, mod6]
    return pl.pallas_call(
        functools.partial(_mm_kernel, n_a=n_a, nk=nk, mode=mode),
        grid=(m // tm, n // tn, nk),
        in_specs=in_specs,
        out_specs=pl.BlockSpec((tm, tn), lambda i, j, k: (i, j)),
        out_shape=jax.ShapeDtypeStruct((m, n), out_dtype),
        compiler_params=_params("arbitrary", "arbitrary", "arbitrary"),
        name="mm_" + mode,
    )(*args)


def _fcum_kernel(fg_ref, bias_ref, fc_ref):
    n_slab = fg_ref.shape[0] // HEAD_DIM
    tril = _tril(HEAD_DIM)

    def body(r, carry):
        rows = pl.ds(pl.multiple_of(r * HEAD_DIM, HEAD_DIM), HEAD_DIM)
        z = fg_ref[rows, :] + bias_ref[...]
        lf = jnp.minimum(z, 0.0) - jnp.log1p(jnp.exp(-jnp.abs(z)))
        f = _tril_cumsum(tril, lf) + carry
        fc_ref[rows, :] = f
        return f[HEAD_DIM - 1:HEAD_DIM, :]

    lax.fori_loop(0, n_slab, body, jnp.zeros((1, HEAD_DIM), F32))


def _fcum(fg3, bias_row):
    b, s, _ = fg3.shape
    return pl.pallas_call(
        _fcum_kernel,
        grid=(b,),
        in_specs=[
            pl.BlockSpec((None, s, HEAD_DIM), lambda i: (i, 0, 0)),
            pl.BlockSpec((1, HEAD_DIM), lambda i: (0, 0)),
        ],
        out_specs=pl.BlockSpec((None, s, HEAD_DIM), lambda i: (i, 0, 0)),
        out_shape=jax.ShapeDtypeStruct((b, s, HEAD_DIM), F32),
        compiler_params=_params("arbitrary"),
        name="fox_fcum",
    )(fg3, bias_row)


V_ROWS = HEAD_DIM + 16


def _fox_kernel(q_ref, k_ref, v_ref, fc_ref, qn_ref, kn_ref, o_ref,
                kaug_s, vaug_s, qaug_s, s0_s, s1_s, m_s, acc_s, *, blk, n_blk):
    h = pl.program_id(1)
    qi = pl.program_id(2)
    qblk = 2 * blk
    lane = lax.broadcasted_iota(jnp.int32, (blk, HEAD_DIM), 1)

    @pl.when(qi == 0)
    def _():
        ones_row = (lax.broadcasted_iota(jnp.int32, (V_ROWS - HEAD_DIM, blk), 0) == 0)
        for c in range(n_blk):
            rows = slice(c * blk, (c + 1) * blk)
            k = k_ref[rows, :]
            ms = jnp.mean(k * k, axis=-1, keepdims=True)
            kaug_s[rows, 0:HEAD_DIM] = (k * lax.rsqrt(ms + EPS) * kn_ref[...]).astype(BF16)
            fcol = jnp.sum(jnp.where(lane == h, fc_ref[rows, :], 0.0), axis=1, keepdims=True)
            nf = fcol * (-LOG2E)
            hi = nf.astype(BF16).astype(F32)
            mid = (nf - hi).astype(BF16).astype(F32)
            lo = (nf - hi) - mid
            aug = jnp.where(lane == 0, hi, jnp.where(lane == 1, mid, jnp.where(lane == 2, lo, 0.0)))
            kaug_s[rows, HEAD_DIM:2 * HEAD_DIM] = aug.astype(BF16)
            vaug_s[c, 0:HEAD_DIM, :] = v_ref[rows, :].T.astype(BF16)
            vaug_s[c, HEAD_DIM:V_ROWS, :] = ones_row.astype(BF16)

    q = q_ref[...]
    ms = jnp.mean(q * q, axis=-1, keepdims=True)
    qaug_s[:, 0:HEAD_DIM] = (q * lax.rsqrt(ms + EPS) * qn_ref[...]
                             * (HEAD_DIM ** -0.5 * LOG2E)).astype(BF16)
    qaug_s[:, HEAD_DIM:2 * HEAD_DIM] = (
        lax.broadcasted_iota(jnp.int32, (qblk, HEAD_DIM), 1) < 3).astype(BF16)
    m_s[...] = jnp.full(m_s.shape, MASK_VALUE, F32)
    acc_s[...] = jnp.zeros(acc_s.shape, F32)

    def scores(kj, q0=0):
        rows = pl.ds(pl.multiple_of(kj * blk, blk), blk)
        return lax.dot_general(kaug_s[rows, :], qaug_s[q0:, :], _NT, preferred_element_type=F32)

    def update(s_ref, kj, causal, q0=0):
        st = s_ref[:, q0:]
        if causal:
            kr = lax.broadcasted_iota(jnp.int32, st.shape, 0)
            qc = lax.broadcasted_iota(jnp.int32, st.shape, 1)
            st = jnp.where(kr <= qc, st, MASK_VALUE)
        m_old = m_s[:, q0:]
        m_new = jnp.maximum(m_old, jnp.max(st, axis=0, keepdims=True))
        pt = jnp.exp2(st - m_new).astype(BF16)
        alpha = jnp.exp2(m_old - m_new)
        acc_s[:, q0:] = alpha * acc_s[:, q0:] + jnp.dot(vaug_s[kj], pt,
                                                        preferred_element_type=F32)
        m_s[:, q0:] = m_new

    s0_s[...] = scores(0)

    def pair(i, carry):
        kj = 2 * i
        s1_s[...] = scores(kj + 1)
        update(s0_s, kj, False)
        s0_s[...] = scores(kj + 2)
        update(s1_s, kj + 1, False)
        return carry

    lax.fori_loop(0, qi, pair, 0)
    s1_s[:, blk:] = scores(2 * qi + 1, blk)
    update(s0_s, 2 * qi, True)
    update(s1_s, 2 * qi + 1, True, blk)

    acc = acc_s[...]
    ot = acc[0:HEAD_DIM, :] / acc[HEAD_DIM:HEAD_DIM + 1, :]
    o_ref[...] = ot.T.astype(o_ref.dtype)


def _fox(proj3, fc, qn_w, kn_w, *, n_heads, q_blk0, k_blk0, v_blk0, blk):
    b, s, _ = proj3.shape
    n_blk = s // blk
    qblk = 2 * blk
    assert s % qblk == 0
    return pl.pallas_call(
        functools.partial(_fox_kernel, blk=blk, n_blk=n_blk),
        grid=(b, n_heads, s // qblk),
        in_specs=[
            pl.BlockSpec((None, qblk, HEAD_DIM), lambda i, h, t: (i, t, q_blk0 + h)),
            pl.BlockSpec((None, s, HEAD_DIM), lambda i, h, t: (i, 0, k_blk0 + h)),
            pl.BlockSpec((None, s, HEAD_DIM), lambda i, h, t: (i, 0, v_blk0 + h)),
            pl.BlockSpec((None, s, HEAD_DIM), lambda i, h, t: (i, 0, 0)),
            pl.BlockSpec((1, HEAD_DIM), lambda i, h, t: (0, 0)),
            pl.BlockSpec((1, HEAD_DIM), lambda i, h, t: (0, 0)),
        ],
        out_specs=pl.BlockSpec((None, qblk, HEAD_DIM), lambda i, h, t: (i, t, h)),
        out_shape=jax.ShapeDtypeStruct((b, s, n_heads * HEAD_DIM), BF16),
        scratch_shapes=[
            pltpu.VMEM((s, 2 * HEAD_DIM), BF16),
            pltpu.VMEM((n_blk, V_ROWS, blk), BF16),
            pltpu.VMEM((qblk, 2 * HEAD_DIM), BF16),
            pltpu.VMEM((blk, qblk), F32),
            pltpu.VMEM((blk, qblk), F32),
            pltpu.VMEM((1, qblk), F32),
            pltpu.VMEM((V_ROWS, qblk), F32),
        ],
        compiler_params=_params("arbitrary", "arbitrary", "arbitrary"),
        name="fox_attn",
    )(proj3, proj3, proj3, fc, qn_w.reshape(1, HEAD_DIM), kn_w.reshape(1, HEAD_DIM))


def _hgrn2_kernel(q_ref, f_ref, i_ref, g_ref, lb_ref, nw_ref, seg_ref, o_ref,
                  st_s, y_s, g_s, kk_s, qf_s, r_s, a_s, qe_s, u_s, dec_s, *, layer, rows):
    t_id = pl.program_id(2)
    n_chunk = rows // CHUNK

    @pl.when(t_id == 0)
    def _():
        st_s[...] = jnp.zeros(st_s.shape, F32)

    depth = lb_ref.shape[0]
    lbs = [lb_ref[j] for j in range(depth)]
    mx = lbs[0]
    for j in range(1, depth):
        mx = jnp.maximum(mx, lbs[j])
    es = [jnp.exp(v - mx) for v in lbs]
    den = es[0]
    for j in range(1, depth):
        den = den + es[j]
    lb = jnp.zeros((1, HEAD_DIM), F32)
    for j in range(1, layer + 1):
        lb = lb + es[j] / den

    tril = _tril(CHUNK)

    half = SUB // 2
    zeros_half = jnp.zeros((half, HEAD_DIM), F32)
    row_half = lax.broadcasted_iota(jnp.int32, (half, HEAD_DIM), 0)

    def prep(c, carry):
        rws = pl.ds(pl.multiple_of(c * CHUNK, CHUNK), CHUNK)
        z = f_ref[rws, :]
        kk = (1.0 - lb) * _sigmoid(-z)
        logf = jnp.log1p(-jnp.minimum(kk, K_MAX))
        gc = _tril_cumsum(tril, logf) * LOG2E
        q = q_ref[rws, :]
        qf = q * _sigmoid(q)
        kk_s[rws, :] = kk
        g_s[rws, :] = gc
        qf_s[rws, :] = qf
        for b in range(N_SUB):
            r0 = b * SUB
            q_t, q_b = qf[r0:r0 + half], qf[r0 + half:r0 + SUB]
            k_t, k_b = kk[r0:r0 + half], kk[r0 + half:r0 + SUB]
            g_t, g_b = gc[r0:r0 + half], gc[r0 + half:r0 + SUB]
            m_t = g_t[half // 2:half // 2 + 1]
            m_b = g_b[half // 2:half // 2 + 1]
            g_8 = g_b[0:1]
            qt_f = q_t * jnp.exp2(g_t - m_t)
            pt_f = k_t * jnp.exp2(m_t - g_t)
            qb_f = q_b * jnp.exp2(g_b - m_b)
            pb_f = k_b * jnp.exp2(m_b - g_b)
            qx_f = q_b * jnp.exp2(g_b - g_8)
            px_f = k_t * jnp.exp2(g_8 - g_t)
            yrows = pl.ds(pl.multiple_of(c * CHUNK + r0, SUB), SUB)
            for s in range(SUB):
                if s < half:
                    top = jnp.where(row_half >= s, qt_f * pt_f[s:s + 1], 0.0)
                    bot = qx_f * px_f[s:s + 1]
                else:
                    top = zeros_half
                    bot = jnp.where(row_half >= s - half, qb_f * pb_f[s - half:s - half + 1], 0.0)
                x = jnp.concatenate([top, bot], axis=0)
                y_s[yrows, s * HEAD_DIM:(s + 1) * HEAD_DIM] = x.astype(BF16)
        return carry

    lax.fori_loop(0, n_chunk, prep, 0, unroll=True)

    r_s[...] = jnp.dot(y_s[...], seg_ref[...], preferred_element_type=F32)

    tr = lax.broadcasted_iota(jnp.int32, (CHUNK, CHUNK), 0)
    tc = lax.broadcasted_iota(jnp.int32, (CHUNK, CHUNK), 1)
    same_blk = (tr // SUB) == (tc // SUB)
    diag_mask = jnp.logical_and(same_blk, tc <= tr)

    def scores(c, carry):
        rws = pl.ds(pl.multiple_of(c * CHUNK, CHUNK), CHUNK)
        qf = qf_s[rws, :]
        kk = kk_s[rws, :]
        gc = g_s[rws, :]
        blocks = [jnp.zeros((SUB, CHUNK), F32)]
        for i in range(1, N_SUB):
            lo = i * SUB
            gn = gc[lo:lo + 1]
            ql = qf[lo:lo + SUB] * jnp.exp2(gc[lo:lo + SUB] - gn)
            kr = jnp.concatenate([kk[:lo] * jnp.exp2(gn - gc[:lo]),
                                  jnp.zeros((CHUNK - lo, HEAD_DIM), F32)], axis=0)
            blocks.append(lax.dot_general(ql.astype(BF16), kr.astype(BF16), _NT,
                                          preferred_element_type=F32))
        a_off = jnp.concatenate(blocks, axis=0)
        a = jnp.where(diag_mask, r_s[rws, :][:, :CHUNK], a_off)
        a_s[rws, :] = a.astype(BF16)
        g_last = gc[CHUNK - 1:CHUNK]
        qe_s[rws, :] = (qf * jnp.exp2(gc)).astype(BF16)
        kd = kk * jnp.exp2(g_last - gc)
        u_s[c] = lax.dot_general(i_ref[rws, :].astype(BF16), kd.astype(BF16), _TN,
                                 preferred_element_type=F32)
        dec_s[c] = jnp.exp2(g_last)
        return carry

    lax.fori_loop(0, n_chunk, scores, 0, unroll=True)

    def chunk(c, carry):
        rws = pl.ds(pl.multiple_of(c * CHUNK, CHUNK), CHUNK)
        vb = i_ref[rws, :].astype(BF16)
        st = st_s[...]
        o = jnp.dot(a_s[rws, :], vb, preferred_element_type=F32)
        o = o + lax.dot_general(qe_s[rws, :], st.astype(BF16), _NT,
                                preferred_element_type=F32)
        st_s[...] = dec_s[c] * st + u_s[c]
        ms = jnp.mean(o * o, axis=-1, keepdims=True)
        y = o * lax.rsqrt(ms + EPS) * nw_ref[...]
        g = g_ref[rws, :]
        o_ref[rws, :] = (y * (g * _sigmoid(g))).astype(o_ref.dtype)
        return carry

    lax.fori_loop(0, n_chunk, chunk, 0, unroll=True)


def _hgrn2(proj3, lower_bounds, nw, *, layer, n_heads, rows):
    b, s, _ = proj3.shape
    depth = lower_bounds.shape[0]
    rows = min(rows, s)
    seg_r = jnp.arange(SUB * HEAD_DIM, dtype=jnp.int32)[:, None] // HEAD_DIM
    seg_c = jnp.arange(HEAD_DIM, dtype=jnp.int32)[None, :]
    seg = jnp.logical_and(seg_c < CHUNK, (seg_c % SUB) == seg_r).astype(BF16)
    lb4 = lower_bounds.astype(F32).reshape(depth, n_heads, 1, HEAD_DIM)

    def col(off):
        return pl.BlockSpec((None, rows, HEAD_DIM), lambda i, h, t: (i, t, off * n_heads + h))

    return pl.pallas_call(
        functools.partial(_hgrn2_kernel, layer=layer, rows=rows),
        grid=(b, n_heads, s // rows),
        in_specs=[
            col(0), col(1), col(2), col(3),
            pl.BlockSpec((depth, None, 1, HEAD_DIM), lambda i, h, t: (0, h, 0, 0)),
            pl.BlockSpec((1, HEAD_DIM), lambda i, h, t: (0, 0)),
            pl.BlockSpec((SUB * HEAD_DIM, HEAD_DIM), lambda i, h, t: (0, 0)),
        ],
        out_specs=pl.BlockSpec((None, rows, HEAD_DIM), lambda i, h, t: (i, t, h)),
        out_shape=jax.ShapeDtypeStruct((b, s, n_heads * HEAD_DIM), BF16),
        scratch_shapes=[
            pltpu.VMEM((HEAD_DIM, HEAD_DIM), F32),
            pltpu.VMEM((rows, SUB * HEAD_DIM), BF16),
            pltpu.VMEM((rows, HEAD_DIM), F32),
            pltpu.VMEM((rows, HEAD_DIM), F32),
            pltpu.VMEM((rows, HEAD_DIM), F32),
            pltpu.VMEM((rows, HEAD_DIM), F32),
            pltpu.VMEM((rows, CHUNK), BF16),
            pltpu.VMEM((rows, HEAD_DIM), BF16),
            pltpu.VMEM((rows // CHUNK, HEAD_DIM, HEAD_DIM), F32),
            pltpu.VMEM((rows // CHUNK, 1, HEAD_DIM), F32),
        ],
        compiler_params=_params("arbitrary", "arbitrary", "arbitrary"),
        name="hgrn2",
    )(proj3, proj3, proj3, proj3, lb4, nw.reshape(1, HEAD_DIM), seg)


def kernel(x, c, lower_bounds, w_ada, b_ada, norm_mix_w, norm_ffn_w, w_in, rec_norm_w,
           fg_bias, q_norm_w, k_norm_w, w_out, w_up, w_down):
    b, s, d = x.shape
    depth = w_ada.shape[0]
    n_heads = d // HEAD_DIM
    h_rec = n_heads // 2
    h_att = n_heads - h_rec
    main_cols = (4 * h_rec + 3 * h_att) * HEAD_DIM
    assert w_in.shape[2] == main_cols + h_att and h_att <= HEAD_DIM
    assert s % CHUNK == 0 and s % HEAD_DIM == 0
    m = b * s
    fox_blk = min(FOX_KEY_BLK, s // 2)

    rows_pad = -(-b // 8) * 8
    c_pad = jnp.zeros((rows_pad, d), F32).at[:b].set(c.astype(F32))
    mod = _adaln_mod(c_pad, w_ada, b_ada)

    w_in_t = jnp.swapaxes(w_in, 1, 2)

    x2 = x.reshape(m, d)
    for l in range(depth):
        mod6 = mod[l, :b].reshape(b * N_MOD, 1, d)

        h, fg3 = _norm_mod(x2.reshape(b, s, d), norm_mix_w[l], mod6, 1, 0,
                           fg_weights=(w_in_t, l, main_cols))
        proj = _matmul_wcast([h.reshape(m, d)], w_in_t, l, mode="plain", out_dtype=F32,
                             tm=MM_TM, tn=MM_TN, n_cols=main_cols, w_t=True)
        proj3 = proj.reshape(b, s, main_cols)

        o_rec = _hgrn2(proj3, lower_bounds, rec_norm_w[l], layer=l, n_heads=h_rec,
                       rows=HGRN2_ROWS)

        bias_row = jnp.pad(fg_bias[l].astype(F32), (0, HEAD_DIM - h_att)).reshape(1, HEAD_DIM)
        fc = _fcum(fg3, bias_row)
        o_att = _fox(proj3, fc, q_norm_w[l], k_norm_w[l], n_heads=h_att,
                     q_blk0=4 * h_rec, k_blk0=4 * h_rec + h_att, v_blk0=4 * h_rec + 2 * h_att,
                     blk=fox_blk)

        x2 = _matmul_wcast([o_rec.reshape(m, -1), o_att.reshape(m, -1)], w_out, l, mode="resid",
                           out_dtype=F32, tm=MM_TM, tn=W_OUT_TN, resid=(x2, mod6, 2),
                           rows_per_batch=s)

        h2 = _norm_mod(x2.reshape(b, s, d), norm_ffn_w[l], mod6, 4, 3).reshape(m, d)
        u, w_down_b = _matmul_wcast([h2], w_up, l, mode="relu2", out_dtype=BF16, tm=MM_TM,
                                    tn=MM_TN, side_cast=(w_down, l))
        x2 = _matmul([u], w_down_b[None], 0, mode="resid", out_dtype=F32, tm=MM_TM, tn=MM_TN,
                     tk=W_DOWN_TK, resid=(x2, mod6, 5), rows_per_batch=s)
    return x2.reshape(b, s, d)
```
